```python
import math
import jax, jax.numpy as jnp
from jax import lax
import numpy as np

D_MODEL = 2048
BATCH = 4
SEQ = 8192
DEPTH = 4

CHUNK = 64
Q_BLOCK = 128
N_MIXERS = 3
FFN_DIM = 5632
ATTN_HEADS = 16
ATTN_QK_DIM = D_MODEL // ATTN_HEADS // 2
ATTN_V_DIM = 2 * ATTN_QK_DIM
CONV_WIDTH = 3
HGRN_HEADS = 16
HGRN_HEAD_DIM = D_MODEL // HGRN_HEADS
N_SUBLAYERS = 3
N_MOD = 3 * N_SUBLAYERS
N_ATTN_LAYERS = (DEPTH + 2) // 3
N_CONV_LAYERS = (DEPTH + 1) // 3
N_HGRN_LAYERS = DEPTH // 3
RMS_EPS = 1e-6

kernel_name = "hybrid_diffattn_shortconv_hgrn2_macaron"


def rms_norm(x, g):
    xf = x.astype(jnp.float32)
    y = xf * lax.rsqrt(jnp.mean(xf * xf, axis=-1, keepdims=True) + RMS_EPS)
    return (y * g.astype(jnp.float32)).astype(x.dtype)


def modulate(h, shift, scale):
    return h * (1 + scale[:, None, :]) + shift[:, None, :]


def swiglu(h, w_gate, w_up, w_down):
    return (jax.nn.silu(h @ w_gate) * (h @ w_up)) @ w_down


def alibi_slopes(n_heads):
    return jnp.asarray(np.array([2.0 ** (-8.0 * (h + 1) / n_heads) for h in range(n_heads)], dtype=np.float32))


def diff_attention(h, w_in, w_out, q_gain, k_gain, lam_vecs, subln_g, lambda_init):
    B, S, _ = h.shape
    H, dh, dv = ATTN_HEADS, ATTN_QK_DIM, ATTN_V_DIM
    q, k, v = jnp.split(h @ w_in, 3, axis=-1)
    q = rms_norm(q.reshape(B, S, H, 2, dh), q_gain)
    k = rms_norm(k.reshape(B, S, H, 2, dh), k_gain)
    v = v.reshape(B, S, H, dv)
    lam = (jnp.exp(jnp.sum(lam_vecs[0].astype(jnp.float32) * lam_vecs[1].astype(jnp.float32)))
           - jnp.exp(jnp.sum(lam_vecs[2].astype(jnp.float32) * lam_vecs[3].astype(jnp.float32)))
           + lambda_init)
    slopes = alibi_slopes(H)
    k_pos = jnp.arange(S)
    n_blk = S // Q_BLOCK
    q_blocks = q.reshape(B, n_blk, Q_BLOCK, H, 2, dh).transpose(1, 0, 2, 3, 4, 5)
    scale = dh ** -0.5

    def attend(args):
        q_blk, blk = args
        q_pos = blk * Q_BLOCK + jnp.arange(Q_BLOCK)
        s = jnp.einsum('bqhjd,bkhjd->bhjqk', q_blk, k, preferred_element_type=jnp.float32) * scale
        dist = jnp.abs(q_pos[:, None] - k_pos[None, :]).astype(jnp.float32)
        s = s - (slopes[:, None, None] * dist)[None, :, None]
        mask = (k_pos[None, :] // CHUNK) <= (q_pos[:, None] // CHUNK)
        p = jax.nn.softmax(jnp.where(mask, s, -jnp.inf), axis=-1)
        a = p[:, :, 0] - lam * p[:, :, 1]
        return jnp.einsum('bhqk,bkhv->bqhv', a.astype(v.dtype), v)

    o = lax.map(attend, (q_blocks, jnp.arange(n_blk)))
    o = o.transpose(1, 0, 2, 3, 4).reshape(B, S, H, dv)
    o = rms_norm(o, subln_g) * (1.0 - lambda_init)
    return o.reshape(B, S, H * dv) @ w_out


def short_conv_mixer(h, w_in, conv_w, w_out):
    b_gate, c_gate, u = jnp.split(h @ w_in, 3, axis=-1)
    v = c_gate * u
    y = lax.conv_general_dilated(
        v, conv_w[:, None, :].astype(v.dtype), window_strides=(1,),
        padding=[(CONV_WIDTH - 1, 0)], dimension_numbers=('NWC', 'WIO', 'NWC'),
        feature_group_count=D_MODEL)
    return (b_gate * y) @ w_out


def hgrn2_mixer(h, w_in, w_out, o_norm_g, lower_bound):
    B, S, _ = h.shape
    H, dk = HGRN_HEADS, HGRN_HEAD_DIM
    q, f_logit, i, g = jnp.split(h @ w_in, 4, axis=-1)
    lb = lower_bound.astype(jnp.float32).reshape(H, dk)
    f = lb + (1.0 - lb) * jax.nn.sigmoid(f_logit.reshape(B, S, H, dk).astype(jnp.float32))
    log_f = jnp.log(f)
    k = 1.0 - f
    n_chunks = S // CHUNK

    def to_chunks(t):
        return t.reshape(B, n_chunks, CHUNK, H, dk).transpose(1, 0, 3, 2, 4).astype(jnp.float32)

    causal = jnp.tril(jnp.ones((CHUNK, CHUNK), dtype=bool))

    def step(state, xs):
        q_c, k_c, i_c, lf_c = xs
        b = jnp.cumsum(lf_c, axis=-2)
        inter = jnp.einsum('bhtk,bhkv->bhtv', q_c * jnp.exp(b), state)
        diff = b[:, :, :, None, :] - b[:, :, None, :, :]
        decay = jnp.exp(jnp.where(causal[:, :, None], diff, -jnp.inf))
        scores = jnp.einsum('bhtsk,bhsk->bhts', q_c[:, :, :, None, :] * decay, k_c)
        intra = jnp.einsum('bhts,bhsv->bhtv', scores, i_c)
        b_last = b[:, :, -1:, :]
        state = (jnp.exp(b_last[:, :, 0, :, None]) * state
                 + jnp.einsum('bhsk,bhsv->bhkv', k_c * jnp.exp(b_last - b), i_c))
        return state, inter + intra

    state0 = jnp.zeros((B, H, dk, dk), jnp.float32)
    _, o = lax.scan(step, state0, (to_chunks(q), to_chunks(k), to_chunks(i), to_chunks(log_f)))
    o = o.transpose(1, 0, 3, 2, 4).reshape(B, S, H, dk)
    o = rms_norm(o, o_norm_g) * jax.nn.silu(g.reshape(B, S, H, dk).astype(jnp.float32))
    return o.reshape(B, S, H * dk).astype(h.dtype) @ w_out


def setup_inputs(seed: int = 0) -> dict:
    key = jax.random.key(seed)
    ks = jax.random.split(key, 24)
    D, F = D_MODEL, FFN_DIM

    def nrm(k, shape, scale):
        return jax.random.normal(k, shape, jnp.float32) * scale

    return {
        "x": nrm(ks[0], (BATCH, SEQ, D), 1.0),
        "c": nrm(ks[1], (BATCH, D), 1.0),
        "ada_w": nrm(ks[2], (DEPTH, D, N_MOD * D), 0.5 * D ** -0.5),
        "ada_b": nrm(ks[3], (DEPTH, N_MOD * D), 0.02),
        "norm_g": 1.0 + nrm(ks[4], (DEPTH, N_SUBLAYERS, D), 0.02),
        "ffn_w_gate": nrm(ks[5], (DEPTH, 2, D, F), D ** -0.5),
        "ffn_w_up": nrm(ks[6], (DEPTH, 2, D, F), D ** -0.5),
        "ffn_w_down": nrm(ks[7], (DEPTH, 2, F, D), F ** -0.5),
        "attn_w_in": nrm(ks[8], (N_ATTN_LAYERS, D, 3 * D), D ** -0.5),
        "attn_w_out": nrm(ks[9], (N_ATTN_LAYERS, D, D), D ** -0.5),
        "attn_q_gain": 1.0 + nrm(ks[10], (N_ATTN_LAYERS, ATTN_QK_DIM), 0.02),
        "attn_k_gain": 1.0 + nrm(ks[11], (N_ATTN_LAYERS, ATTN_QK_DIM), 0.02),
        "attn_lambda": nrm(ks[12], (N_ATTN_LAYERS, 4, ATTN_QK_DIM), 0.1),
        "attn_subln_g": 1.0 + nrm(ks[13], (N_ATTN_LAYERS, ATTN_V_DIM), 0.02),
        "conv_w_in": nrm(ks[14], (N_CONV_LAYERS, D, 3 * D), D ** -0.5),
        "conv_w": nrm(ks[15], (N_CONV_LAYERS, CONV_WIDTH, D), CONV_WIDTH ** -0.5),
        "conv_w_out": nrm(ks[16], (N_CONV_LAYERS, D, D), D ** -0.5),
        "hgrn_w_in": nrm(ks[17], (N_HGRN_LAYERS, D, 4 * D), D ** -0.5),
        "hgrn_w_out": nrm(ks[18], (N_HGRN_LAYERS, D, D), D ** -0.5),
        "hgrn_o_norm_g": 1.0 + nrm(ks[19], (N_HGRN_LAYERS, HGRN_HEAD_DIM), 0.02),
        "hgrn_lb_logits": nrm(ks[20], (DEPTH, D), 0.1),
    }


def reference(x, c, ada_w, ada_b, norm_g, ffn_w_gate, ffn_w_up, ffn_w_down,
              attn_w_in, attn_w_out, attn_q_gain, attn_k_gain, attn_lambda, attn_subln_g,
              conv_w_in, conv_w, conv_w_out,
              hgrn_w_in, hgrn_w_out, hgrn_o_norm_g, hgrn_lb_logits):
    cond = jax.nn.silu(c)
    lb_cum = jnp.cumsum(jax.nn.softmax(hgrn_lb_logits.astype(jnp.float32), axis=0), axis=0)
    lb_all = lb_cum - lb_cum[0:1]
    for layer in range(DEPTH):
        mod = (cond @ ada_w[layer] + ada_b[layer]).reshape(-1, N_MOD, D_MODEL)
        sh1, sc1, g1, sh2, sc2, g2, sh3, sc3, g3 = [mod[:, j] for j in range(N_MOD)]
        h = modulate(rms_norm(x, norm_g[layer, 0]), sh1, sc1)
        x = x + 0.5 * g1[:, None, :] * swiglu(h, ffn_w_gate[layer, 0], ffn_w_up[layer, 0], ffn_w_down[layer, 0])
        h = modulate(rms_norm(x, norm_g[layer, 1]), sh2, sc2)
        kind, slot = layer % N_MIXERS, layer // N_MIXERS
        if kind == 0:
            y = diff_attention(h, attn_w_in[slot], attn_w_out[slot], attn_q_gain[slot], attn_k_gain[slot],
                               attn_lambda[slot], attn_subln_g[slot], 0.8 - 0.6 * math.exp(-0.3 * layer))
        elif kind == 1:
            y = short_conv_mixer(h, conv_w_in[slot], conv_w[slot], conv_w_out[slot])
        else:
            y = hgrn2_mixer(h, hgrn_w_in[slot], hgrn_w_out[slot], hgrn_o_norm_g[slot], lb_all[layer])
        x = x + g2[:, None, :] * y
        h = modulate(rms_norm(x, norm_g[layer, 2]), sh3, sc3)
        x = x + 0.5 * g3[:, None, :] * swiglu(h, ffn_w_gate[layer, 1], ffn_w_up[layer, 1], ffn_w_down[layer, 1])
    return x
```

```python
import functools
import math

import jax
import jax.numpy as jnp
from jax import lax
from jax.experimental import pallas as pl
from jax.experimental.pallas import tpu as pltpu

F32 = jnp.float32
BF16 = jnp.bfloat16

RMS_EPS = 1e-6
CHUNK = 64
N_MIXERS = 3
N_MOD = 9
CONV_WIDTH = 3
LOG2E = 1.4426950408889634
NEG_BIG = -1e30

LANES = 128
SUBLANES = 8
V7X_VMEM_LIMIT_BYTES = 56 * 1024 * 1024


def _tile(n, preferred):
    for t in (preferred, 2048, 1024, 512, 256, 128, 64, 32, 16, 8):
        if t <= preferred and n % t == 0:
            return t
    return n


def _params(*sem):
    return pltpu.CompilerParams(dimension_semantics=sem,
                                vmem_limit_bytes=V7X_VMEM_LIMIT_BYTES)


def _norm_modulate(x, g, shift, scale):
    ms = jnp.mean(x * x, axis=-1, keepdims=True)
    return (x * lax.rsqrt(ms + RMS_EPS)) * g * (1.0 + scale) + shift


def _ada_kernel(c_ref, w_ref, b_ref, o_ref):
    c = c_ref[...]
    cond = (c * jax.nn.sigmoid(c)).astype(BF16)
    o_ref[0] = jnp.dot(cond, w_ref[0].astype(BF16),
                       preferred_element_type=F32) + b_ref[0]


def _ada_mod(c, ada_w, ada_b):
    depth, d, n = ada_w.shape
    bsz = c.shape[0]
    rows = -(-bsz // SUBLANES) * SUBLANES
    c_pad = jnp.zeros((rows, d), F32).at[:bsz].set(c)
    tn = _tile(n, 1024)
    out = pl.pallas_call(
        _ada_kernel,
        grid=(depth, n // tn),
        in_specs=[pl.BlockSpec((rows, d), lambda l, j: (0, 0)),
                  pl.BlockSpec((1, d, tn), lambda l, j: (l, 0, j)),
                  pl.BlockSpec((1, 1, tn), lambda l, j: (l, 0, j))],
        out_specs=pl.BlockSpec((1, rows, tn), lambda l, j: (l, 0, j)),
        out_shape=jax.ShapeDtypeStruct((depth, rows, n), F32),
        compiler_params=_params("parallel", "parallel"),
        name="ada_mod",
    )(c_pad, ada_w, ada_b.reshape(depth, 1, n))
    return out[:, :bsz].reshape(depth, bsz, N_MOD, d)


def _ffn_kernel(x_ref, mod_ref, g_ref, wg_ref, wu_ref, wd_ref, o_ref, h_ref, *, sub):
    f = pl.program_id(2)
    r = 3 * sub

    @pl.when(f == 0)
    def _():
        h = _norm_modulate(x_ref[0], g_ref[sub:sub + 1, :],
                           mod_ref[0, r:r + 1, :], mod_ref[0, r + 1:r + 2, :])
        h_ref[...] = h.astype(BF16)
        o_ref[0] = jnp.zeros(o_ref.shape[1:], F32)

    h = h_ref[...]
    a = jnp.dot(h, wg_ref[...], preferred_element_type=F32)
    b = jnp.dot(h, wu_ref[...], preferred_element_type=F32)
    p = (a * jax.nn.sigmoid(a) * b).astype(BF16)
    o_ref[0] += jnp.dot(p, wd_ref[...], preferred_element_type=F32)

    @pl.when(f == pl.num_programs(2) - 1)
    def _():
        o_ref[0] = x_ref[0] + (0.5 * mod_ref[0, r + 2:r + 3, :]) * o_ref[0]


def _ffn(x, mod, norm_g, w_gate, w_up, w_down, sub):
    bsz, s, d = x.shape
    f = w_gate.shape[1]
    tm = _tile(s, 512)
    tf = _tile(f, 512)
    return pl.pallas_call(
        functools.partial(_ffn_kernel, sub=sub),
        grid=(bsz, s // tm, f // tf),
        in_specs=[pl.BlockSpec((1, tm, d), lambda b, i, j: (b, i, 0)),
                  pl.BlockSpec((1, N_MOD, d), lambda b, i, j: (b, 0, 0)),
                  pl.BlockSpec((3, d), lambda b, i, j: (0, 0)),
                  pl.BlockSpec((d, tf), lambda b, i, j: (0, j)),
                  pl.BlockSpec((d, tf), lambda b, i, j: (0, j)),
                  pl.BlockSpec((tf, d), lambda b, i, j: (j, 0))],
        out_specs=pl.BlockSpec((1, tm, d), lambda b, i, j: (b, i, 0)),
        out_shape=jax.ShapeDtypeStruct((bsz, s, d), F32),
        scratch_shapes=[pltpu.VMEM((tm, d), BF16)],
        compiler_params=_params("parallel", "parallel", "arbitrary"),
        name="ffn",
    )(x, mod, norm_g, w_gate, w_up, w_down)


def _qk_norm_store(acc, gain, o_ref, post_scale):
    half = LANES // 2
    lo = lax.broadcasted_iota(jnp.int32, (1, LANES), 1) < half
    for j in range(acc.shape[1] // LANES):
        y = acc[:, j * LANES:(j + 1) * LANES]
        y2 = y * y
        s_lo = jnp.sum(jnp.where(lo, y2, 0.0), axis=-1, keepdims=True)
        s_hi = jnp.sum(jnp.where(lo, 0.0, y2), axis=-1, keepdims=True)
        ms = jnp.where(lo, s_lo, s_hi) * (1.0 / half)
        yn = (y * lax.rsqrt(ms + RMS_EPS)) * gain
        if post_scale != 1.0:
            yn = yn * post_scale
        o_ref[0, :, j * LANES:(j + 1) * LANES] = yn.astype(o_ref.dtype)


def _proj_in_kernel(x_ref, mod_ref, g_ref, w_ref, *rest, qk_norm, q_scale, tiles_per_section):
    if qk_norm:
        gains_ref, o_ref, h_ref = rest
    else:
        o_ref, h_ref = rest
    n = pl.program_id(2)

    @pl.when(n == 0)
    def _():
        h = _norm_modulate(x_ref[0], g_ref[1:2, :], mod_ref[0, 3:4, :], mod_ref[0, 4:5, :])
        h_ref[...] = h.astype(BF16)

    acc = jnp.dot(h_ref[...], w_ref[...], preferred_element_type=F32)
    if not qk_norm:
        o_ref[0] = acc.astype(o_ref.dtype)
        return

    section = n // tiles_per_section

    @pl.when(section == 0)
    def _():
        _qk_norm_store(acc, gains_ref[0:1, :], o_ref, q_scale)

    @pl.when(section == 1)
    def _():
        _qk_norm_store(acc, gains_ref[1:2, :], o_ref, 1.0)

    @pl.when(section == 2)
    def _():
        o_ref[0] = acc.astype(o_ref.dtype)


def _proj_in(x, mod, norm_g, w, out_dtype, gains=None, q_scale=1.0):
    bsz, s, d = x.shape
    n = w.shape[1]
    tm = _tile(s, 1024)
    tn = _tile(d, 512)
    qk_norm = gains is not None
    in_specs = [pl.BlockSpec((1, tm, d), lambda b, i, j: (b, i, 0)),
                pl.BlockSpec((1, N_MOD, d), lambda b, i, j: (b, 0, 0)),
                pl.BlockSpec((3, d), lambda b, i, j: (0, 0)),
                pl.BlockSpec((d, tn), lambda b, i, j: (0, j))]
    args = [x, mod, norm_g, w]
    if qk_norm:
        in_specs.append(pl.BlockSpec((2, LANES), lambda b, i, j: (0, 0)))
        args.append(gains)
    return pl.pallas_call(
        functools.partial(_proj_in_kernel, qk_norm=qk_norm, q_scale=q_scale,
                          tiles_per_section=d // tn),
        grid=(bsz, s // tm, n // tn),
        in_specs=in_specs,
        out_specs=pl.BlockSpec((1, tm, tn), lambda b, i, j: (b, i, j)),
        out_shape=jax.ShapeDtypeStruct((bsz, s, n), out_dtype),
        scratch_shapes=[pltpu.VMEM((tm, d), BF16)],
        compiler_params=_params("parallel", "parallel", "arbitrary"),
        name="proj_in",
    )(*args)


def _proj_out_kernel(y_ref, w_ref, x_ref, mod_ref, o_ref):
    o_ref[0] = x_ref[0] + mod_ref[0, 5:6, :] * jnp.dot(
        y_ref[0], w_ref[...], preferred_element_type=F32)


def _proj_out(y, w, x, mod):
    bsz, s, d = x.shape
    k = y.shape[2]
    tm = _tile(s, 1024)
    tn = _tile(d, 512)
    return pl.pallas_call(
        _proj_out_kernel,
        grid=(bsz, s // tm, d // tn),
        in_specs=[pl.BlockSpec((1, tm, k), lambda b, i, j: (b, i, 0)),
                  pl.BlockSpec((k, tn), lambda b, i, j: (0, j)),
                  pl.BlockSpec((1, tm, tn), lambda b, i, j: (b, i, j)),
                  pl.BlockSpec((1, N_MOD, tn), lambda b, i, j: (b, 0, j))],
        out_specs=pl.BlockSpec((1, tm, tn), lambda b, i, j: (b, i, j)),
        out_shape=jax.ShapeDtypeStruct((bsz, s, d), F32),
        compiler_params=_params("parallel", "parallel", "parallel"),
        name="proj_out",
    )(y, w, x, mod)


def _attn_kernel(q_ref, k_ref, v_ref, slope_ref, lam_ref, subg_ref, dist_ref, mask_ref,
                 o_ref, vt_ref, acc0_ref, acc1_ref, *, tq, tk, lambda_init):
    qi = pl.program_id(2)
    n_kv = vt_ref.shape[0]
    n_diag = tq // tk
    nt_dims = (((1,), (1,)), ((), ()))

    @pl.when(qi == 0)
    def _():
        for c in range(n_kv):
            blk = v_ref[0, c * tk:(c + 1) * tk, :].astype(F32)
            vt_ref[c] = blk.T.astype(BF16)

    q = q_ref[0]
    lo = lax.broadcasted_iota(jnp.int32, (1, LANES), 1) < LANES // 2
    zero = jnp.zeros_like(q)
    q_maps = (jnp.where(lo, q, zero), jnp.where(lo, zero, q))
    slope2 = slope_ref[0]
    row_bias = lax.broadcasted_iota(jnp.int32, (tk, tq), 0).astype(F32) * slope2
    acc_refs = (acc0_ref, acc1_ref)
    acc0_ref[...] = jnp.zeros_like(acc0_ref)
    acc1_ref[...] = jnp.zeros_like(acc1_ref)

    def update(z, shift, state, acc_ref, vt):
        m, l = state
        m_new = jnp.maximum(m, jnp.max(z, axis=0, keepdims=True) + shift)
        alpha = jnp.exp2(m - m_new)
        p = jnp.exp2(z - (m_new - shift))
        l_new = alpha * l + jnp.sum(p, axis=0, keepdims=True)
        acc_ref[...] = alpha * acc_ref[...] + jnp.dot(
            vt, p.astype(BF16), preferred_element_type=F32)
        return m_new, l_new

    def full_tile(j, carry):
        k = k_ref[0, pl.ds(pl.multiple_of(j * tk, tk), tk), :]
        vt = vt_ref[j]
        shift = slope2 * (j * tk - qi * tq).astype(F32)
        out = []
        for mp in range(2):
            s = lax.dot_general(k, q_maps[mp], nt_dims, preferred_element_type=F32)
            out.append(update(s + row_bias, shift, carry[mp], acc_refs[mp], vt))
        return tuple(out)

    init = (jnp.full((1, tq), NEG_BIG, F32), jnp.zeros((1, tq), F32))
    carry = lax.fori_loop(0, qi * n_diag, full_tile, (init, init))

    for jj in range(n_diag):
        j = qi * n_diag + jj
        k = k_ref[0, pl.ds(pl.multiple_of(j * tk, tk), tk), :]
        vt = vt_ref[j]
        new = []
        for mp in range(2):
            s = lax.dot_general(k, q_maps[mp], nt_dims, preferred_element_type=F32)
            z = s + slope2 * dist_ref[jj] + mask_ref[jj]
            new.append(update(z, 0.0, carry[mp], acc_refs[mp], vt))
        carry = tuple(new)

    lam_v = lam_ref[...]
    lam = (jnp.exp(jnp.sum(lam_v[0:1] * lam_v[1:2], axis=-1, keepdims=True))
           - jnp.exp(jnp.sum(lam_v[2:3] * lam_v[3:4], axis=-1, keepdims=True))
           + lambda_init)
    (_, l0), (_, l1) = carry
    o = acc0_ref[...] / l0 - lam * (acc1_ref[...] / l1)
    ms = jnp.mean(o * o, axis=0, keepdims=True)
    on = (o * lax.rsqrt(ms + RMS_EPS)).T
    o_ref[0] = ((on * subg_ref[...]) * (1.0 - lambda_init)).astype(o_ref.dtype)


def _attn_tables(tq, tk):
    n_diag = tq // tk
    r = (jnp.arange(tk)[None, :, None] + tk * jnp.arange(n_diag)[:, None, None])
    c = jnp.arange(tq)[None, None, :]
    dist = jnp.minimum(r, 2 * c - r).astype(F32)
    mask = jnp.where(r // CHUNK <= c // CHUNK, 0.0, NEG_BIG).astype(F32)
    return dist, jnp.broadcast_to(mask, dist.shape)


def _attention(qkv, slopes2, lam_vecs, subln_g, lambda_init, n_heads):
    bsz, s, d3 = qkv.shape
    d = d3 // 3
    dv = d // n_heads
    assert dv == LANES
    tq = _tile(s, 512)
    tk = tq
    dist, mask = _attn_tables(tq, tk)
    slope_rows = jnp.broadcast_to(slopes2[:, None, None], (n_heads, 1, tq)).astype(F32)
    n_diag = tq // tk
    return pl.pallas_call(
        functools.partial(_attn_kernel, tq=tq, tk=tk, lambda_init=lambda_init),
        grid=(bsz, n_heads, s // tq),
        in_specs=[pl.BlockSpec((1, tq, LANES), lambda b, h, i: (b, i, h)),
                  pl.BlockSpec((1, s, LANES), lambda b, h, i: (b, 0, n_heads + h)),
                  pl.BlockSpec((1, s, LANES), lambda b, h, i: (b, 0, 2 * n_heads + h)),
                  pl.BlockSpec((1, 1, tq), lambda b, h, i: (h, 0, 0)),
                  pl.BlockSpec(lam_vecs.shape, lambda b, h, i: (0, 0)),
                  pl.BlockSpec((1, LANES), lambda b, h, i: (0, 0)),
                  pl.BlockSpec((n_diag, tk, tq), lambda b, h, i: (0, 0, 0)),
                  pl.BlockSpec((n_diag, tk, tq), lambda b, h, i: (0, 0, 0))],
        out_specs=pl.BlockSpec((1, tq, LANES), lambda b, h, i: (b, i, h)),
        out_shape=jax.ShapeDtypeStruct((bsz, s, d), BF16),
        scratch_shapes=[pltpu.VMEM((s // tk, LANES, tk), BF16),
                        pltpu.VMEM((LANES, tq), F32),
                        pltpu.VMEM((LANES, tq), F32)],
        compiler_params=_params("parallel", "parallel", "arbitrary"),
        name="diff_attn",
    )(qkv, qkv, qkv, slope_rows, lam_vecs, subln_g.reshape(1, LANES), dist, mask)


def _conv_kernel(b_ref, c_ref, u_ref, ch_ref, uh_ref, w_ref, o_ref, ext_ref):
    i = pl.program_id(1)
    tm = b_ref.shape[1]
    halo = ch_ref[0] * uh_ref[0]
    ext_ref[0:SUBLANES, :] = jnp.where(i == 0, 0.0, halo)
    v = c_ref[0] * u_ref[0]
    ext_ref[SUBLANES:, :] = v
    v1 = ext_ref[SUBLANES - 1:SUBLANES - 1 + tm, :]
    v2 = ext_ref[SUBLANES - 2:SUBLANES - 2 + tm, :]
    w = w_ref[...]
    y = w[0:1] * v2 + w[1:2] * v1 + w[2:3] * v
    o_ref[0] = (b_ref[0] * y).astype(o_ref.dtype)


def _conv_core(bcu, conv_w):
    bsz, s, d3 = bcu.shape
    d = d3 // 3
    tm = _tile(s, 512)
    tc = _tile(d, 512)
    nc = d // tc
    hb = tm // SUBLANES

    def halo_map(off):
        return lambda b, i, j: (b, jnp.maximum(i * hb - 1, 0), off + j)

    return pl.pallas_call(
        _conv_kernel,
        grid=(bsz, s // tm, nc),
        in_specs=[pl.BlockSpec((1, tm, tc), lambda b, i, j: (b, i, j)),
                  pl.BlockSpec((1, tm, tc), lambda b, i, j: (b, i, nc + j)),
                  pl.BlockSpec((1, tm, tc), lambda b, i, j: (b, i, 2 * nc + j)),
                  pl.BlockSpec((1, SUBLANES, tc), halo_map(nc)),
                  pl.BlockSpec((1, SUBLANES, tc), halo_map(2 * nc)),
                  pl.BlockSpec((CONV_WIDTH, tc), lambda b, i, j: (0, j))],
        out_specs=pl.BlockSpec((1, tm, tc), lambda b, i, j: (b, i, j)),
        out_shape=jax.ShapeDtypeStruct((bsz, s, d), BF16),
        scratch_shapes=[pltpu.VMEM((tm + SUBLANES, tc), F32)],
        compiler_params=_params("parallel", "parallel", "parallel"),
        name="short_conv",
    )(bcu, bcu, bcu, bcu, bcu, conv_w)


def _split3(x):
    hi = x.astype(BF16)
    r1 = x - hi.astype(F32)
    mid = r1.astype(BF16)
    lo = (r1 - mid.astype(F32)).astype(BF16)
    return hi, mid, lo


def _hgrn_kernel(q_ref, fl_ref, i_ref, g_ref, lbl_ref, ong_ref, o_ref,
                 state_ref, b_ref, *, layer, ts):
    t = pl.program_id(2)
    nt_dims = (((1,), (1,)), ((), ()))
    tn_dims = (((0,), (0,)), ((), ()))
    c64 = CHUNK
    sb = SUBLANES

    @pl.when(t == 0)
    def _():
        state_ref[...] = jnp.zeros_like(state_ref)

    lbl = lbl_ref[...]
    e = jnp.exp(lbl - jnp.max(lbl, axis=0, keepdims=True))
    sm = e / jnp.sum(e, axis=0, keepdims=True)
    lb = jnp.zeros((1, LANES), F32)
    for r in range(1, layer + 1):
        lb = lb + sm[r:r + 1]

    row = lax.broadcasted_iota(jnp.int32, (c64, c64), 0)
    col = lax.broadcasted_iota(jnp.int32, (c64, c64), 1)
    tri = jnp.where(row >= col, 1.0, 0.0).astype(BF16)
    ones = jnp.ones((LANES, LANES), BF16)
    sub_row = lax.broadcasted_iota(jnp.int32, (sb, LANES), 0)
    col8 = lax.broadcasted_iota(jnp.int32, (sb, c64), 1)
    level_masks = []
    for grp in (2 * sb, 4 * sb, 8 * sb):
        level_masks.append((row // grp == col // grp)
                           & (row % grp >= grp // 2) & (col % grp < grp // 2))

    def chunk(c, _):
        r0 = pl.multiple_of(c * c64, c64)
        q = q_ref[0, pl.ds(r0, c64), :]
        fl = fl_ref[0, pl.ds(r0, c64), :]
        iv = i_ref[0, pl.ds(r0, c64), :]
        gv = g_ref[0, pl.ds(r0, c64), :]
        f = lb + (1.0 - lb) * jax.nn.sigmoid(fl)
        lf = jnp.log(f)
        kk = 1.0 - f
        i16 = iv.astype(BF16)

        b = jnp.zeros((c64, LANES), F32)
        for piece in _split3(lf):
            b = b + jnp.dot(tri, piece, preferred_element_type=F32)
        b_ref[...] = b
        b_last = b_ref[c64 - 1:c64, :]

        st = state_ref[...]
        inter = lax.dot_general((q * jnp.exp(b)).astype(BF16), st.astype(BF16), nt_dims,
                                preferred_element_type=F32)

        w_rows = []
        for blk in range(c64 // sb):
            b_blk = b[blk * sb:(blk + 1) * sb]
            q_blk = q[blk * sb:(blk + 1) * sb]
            for s_ in range(sb):
                src = blk * sb + s_
                b_s = b_ref[src:src + 1, :]
                k_s = kk[src:src + 1, :]
                dec = jnp.exp(jnp.minimum(b_blk - b_s, 0.0))
                w_rows.append(jnp.where(sub_row >= s_, q_blk * dec * k_s, 0.0))
        w_all = jnp.concatenate(w_rows, axis=0).astype(BF16)
        sums = jnp.dot(w_all, ones, preferred_element_type=F32)
        score_rows = []
        for blk in range(c64 // sb):
            acc = jnp.zeros((sb, c64), F32)
            for s_ in range(sb):
                idx = blk * sb + s_
                acc = jnp.where(col8 == idx, sums[idx * sb:(idx + 1) * sb, :c64], acc)
            score_rows.append(acc)
        scores = jnp.concatenate(score_rows, axis=0)

        for lvl, grp in enumerate((2 * sb, 4 * sb, 8 * sb)):
            parts = []
            for gidx in range(c64 // grp):
                edge = gidx * grp + grp // 2 - 1
                parts.append(jnp.broadcast_to(b_ref[edge:edge + 1, :], (grp, LANES)))
            ref = parts[0] if len(parts) == 1 else jnp.concatenate(parts, axis=0)
            qd = (q * jnp.exp(jnp.minimum(b - ref, 0.0))).astype(BF16)
            kd = (kk * jnp.exp(jnp.minimum(ref - b, 0.0))).astype(BF16)
            blk_scores = lax.dot_general(qd, kd, nt_dims, preferred_element_type=F32)
            scores = scores + jnp.where(level_masks[lvl], blk_scores, 0.0)

        intra = jnp.dot(scores.astype(BF16), i16, preferred_element_type=F32)

        kd_end = (kk * jnp.exp(b_last - b)).astype(BF16)
        upd = lax.dot_general(i16, kd_end, tn_dims, preferred_element_type=F32)
        state_ref[...] = st * jnp.exp(b_last) + upd

        o = inter + intra
        ms = jnp.mean(o * o, axis=-1, keepdims=True)
        on = (o * lax.rsqrt(ms + RMS_EPS)) * ong_ref[...]
        o_ref[0, pl.ds(r0, c64), :] = (on * (gv * jax.nn.sigmoid(gv))).astype(o_ref.dtype)
        return 0

    lax.fori_loop(0, ts // c64, chunk, 0)


def _hgrn_core(proj, lb_logits, o_norm_g, layer, n_heads):
    bsz, s, d4 = proj.shape
    d = d4 // 4
    assert d // n_heads == LANES
    ts = _tile(s, 512)
    depth = lb_logits.shape[0]
    return pl.pallas_call(
        functools.partial(_hgrn_kernel, layer=layer, ts=ts),
        grid=(bsz, n_heads, s // ts),
        in_specs=[pl.BlockSpec((1, ts, LANES), lambda b, h, t: (b, t, h)),
                  pl.BlockSpec((1, ts, LANES), lambda b, h, t: (b, t, n_heads + h)),
                  pl.BlockSpec((1, ts, LANES), lambda b, h, t: (b, t, 2 * n_heads + h)),
                  pl.BlockSpec((1, ts, LANES), lambda b, h, t: (b, t, 3 * n_heads + h)),
                  pl.BlockSpec((depth, LANES), lambda b, h, t: (0, h)),
                  pl.BlockSpec((1, LANES), lambda b, h, t: (0, 0))],
        out_specs=pl.BlockSpec((1, ts, LANES), lambda b, h, t: (b, t, h)),
        out_shape=jax.ShapeDtypeStruct((bsz, s, d), BF16),
        scratch_shapes=[pltpu.VMEM((LANES, LANES), F32),
                        pltpu.VMEM((CHUNK, LANES), F32)],
        compiler_params=_params("parallel", "parallel", "arbitrary"),
        name="hgrn2",
    )(proj, proj, proj, proj, lb_logits, o_norm_g.reshape(1, LANES))


def kernel(x, c, ada_w, ada_b, norm_g, ffn_w_gate, ffn_w_up, ffn_w_down,
           attn_w_in, attn_w_out, attn_q_gain, attn_k_gain, attn_lambda, attn_subln_g,
           conv_w_in, conv_w, conv_w_out,
           hgrn_w_in, hgrn_w_out, hgrn_o_norm_g, hgrn_lb_logits):
    depth = ada_w.shape[0]
    d = x.shape[-1]
    qk_dim = attn_q_gain.shape[-1]
    attn_heads = d // (2 * qk_dim)
    hgrn_heads = d // hgrn_o_norm_g.shape[-1]
    bf = lambda w: w.astype(BF16)

    mod = _ada_mod(c, ada_w, ada_b)
    slopes2 = jnp.asarray(
        [2.0 ** (-8.0 * (h + 1) / attn_heads) * LOG2E for h in range(attn_heads)], F32)

    for layer in range(depth):
        m = mod[layer]
        g = norm_g[layer]
        x = _ffn(x, m, g, bf(ffn_w_gate[layer, 0]), bf(ffn_w_up[layer, 0]),
                 bf(ffn_w_down[layer, 0]), sub=0)
        kind, slot = layer % N_MIXERS, layer // N_MIXERS
        if kind == 0:
            lambda_init = 0.8 - 0.6 * math.exp(-0.3 * layer)
            gains = jnp.stack([jnp.tile(attn_q_gain[slot], 2), jnp.tile(attn_k_gain[slot], 2)])
            qkv = _proj_in(x, m, g, bf(attn_w_in[slot]), BF16, gains=gains,
                           q_scale=qk_dim ** -0.5 * LOG2E)
            y = _attention(qkv, slopes2, attn_lambda[slot], attn_subln_g[slot],
                           lambda_init, attn_heads)
            x = _proj_out(y, bf(attn_w_out[slot]), x, m)
        elif kind == 1:
            bcu = _proj_in(x, m, g, bf(conv_w_in[slot]), F32)
            y = _conv_core(bcu, conv_w[slot])
            x = _proj_out(y, bf(conv_w_out[slot]), x, m)
        else:
            proj = _proj_in(x, m, g, bf(hgrn_w_in[slot]), F32)
            y = _hgrn_core(proj, hgrn_lb_logits, hgrn_o_norm_g[slot], layer, hgrn_heads)
            x = _proj_out(y, bf(hgrn_w_out[slot]), x, m)
        x = _ffn(x, m, g, bf(ffn_w_gate[layer, 1]), bf(ffn_w_up[layer, 1]),
                 bf(ffn_w_down[layer, 1]), sub=2)
    return x
```

```python
import functools
import math

import jax
import jax.numpy as jnp
from jax import lax
from jax.experimental import pallas as pl
from jax.experimental.pallas import tpu as pltpu

F32 = jnp.float32
BF16 = jnp.bfloat16

RMS_EPS = 1e-6
CHUNK = 64
N_MIXERS = 3
N_MOD = 9
CONV_WIDTH = 3
LOG2E = 1.4426950408889634
NEG_BIG = -1e30
F32_MIN_EXP = 127
F32_MANT_BITS = 24

LANES = 128
SUBLANES = 8
BF16_SUBLANES = 16
NORM_ROW_BLOCK = 16
V7X_VMEM_LIMIT_BYTES = 56 * 1024 * 1024

NT_DIMS = (((1,), (1,)), ((), ()))
TN_DIMS = (((0,), (0,)), ((), ()))


def _tile(n, preferred):
    for t in (preferred, 2048, 1024, 512, 256, 128, 64, 32, 16, 8):
        if t <= preferred and n % t == 0:
            return t
    return n


def _params(*sem):
    return pltpu.CompilerParams(dimension_semantics=sem,
                                vmem_limit_bytes=V7X_VMEM_LIMIT_BYTES)


def _norm_modulate_into(h_ref, x_ref, g, shift, scale):
    gain = g * (1.0 + scale)
    rows = h_ref.shape[0]
    blk = min(rows, NORM_ROW_BLOCK)
    for r in range(0, rows, blk):
        x = x_ref[0, r:r + blk, :]
        ms = jnp.mean(x * x, axis=-1, keepdims=True)
        h_ref[r:r + blk, :] = ((x * lax.rsqrt(ms + RMS_EPS)) * gain + shift).astype(BF16)


def _split3(x):
    hi = x.astype(BF16)
    r1 = x - hi.astype(F32)
    mid = r1.astype(BF16)
    lo = (r1 - mid.astype(F32)).astype(BF16)
    return hi, mid, lo


def _ada_kernel(c_ref, w_ref, b_ref, o_ref):
    c = c_ref[...]
    cond = (c * jax.nn.sigmoid(c)).astype(BF16)
    o_ref[0] = jnp.dot(cond, w_ref[0].astype(BF16),
                       preferred_element_type=F32) + b_ref[0]


def _ada_mod(c, ada_w, ada_b):
    depth, d, n = ada_w.shape
    bsz = c.shape[0]
    rows = -(-bsz // SUBLANES) * SUBLANES
    c_pad = jnp.zeros((rows, d), F32).at[:bsz].set(c)
    tn = _tile(n, 1024)
    out = pl.pallas_call(
        _ada_kernel,
        grid=(depth, n // tn),
        in_specs=[pl.BlockSpec((rows, d), lambda l, j: (0, 0)),
                  pl.BlockSpec((1, d, tn), lambda l, j: (l, 0, j)),
                  pl.BlockSpec((1, 1, tn), lambda l, j: (l, 0, j))],
        out_specs=pl.BlockSpec((1, rows, tn), lambda l, j: (l, 0, j)),
        out_shape=jax.ShapeDtypeStruct((depth, rows, n), F32),
        compiler_params=_params("parallel", "parallel"),
        name="ada_mod",
    )(c_pad, ada_w, ada_b.reshape(depth, 1, n))
    return out[:, :bsz].reshape(depth, bsz, N_MOD, d)


def _ffn_kernel(x_ref, mod_ref, g_ref, wg_ref, wu_ref, wd_ref, o_ref, h_ref, *, sub, row_groups):
    f = pl.program_id(2)
    r = 3 * sub

    @pl.when(f == 0)
    def _():
        _norm_modulate_into(h_ref, x_ref, g_ref[sub:sub + 1, :],
                            mod_ref[0, r:r + 1, :], mod_ref[0, r + 1:r + 2, :])
        o_ref[0] = jnp.zeros(o_ref.shape[1:], F32)

    rows = h_ref.shape[0] // row_groups
    slices = [slice(grp * rows, (grp + 1) * rows) for grp in range(row_groups)]
    gated = []
    for sl in slices:
        h = h_ref[sl, :]
        a = jnp.dot(h, wg_ref[...], preferred_element_type=F32)
        b = jnp.dot(h, wu_ref[...], preferred_element_type=F32)
        gated.append((a * jax.nn.sigmoid(a) * b).astype(BF16))
    for sl, p in zip(slices, gated):
        o_ref[0, sl, :] += jnp.dot(p, wd_ref[...], preferred_element_type=F32)

    @pl.when(f == pl.num_programs(2) - 1)
    def _():
        o_ref[0] = x_ref[0] + (0.5 * mod_ref[0, r + 2:r + 3, :]) * o_ref[0]


def _ffn(x, mod, norm_g, w_gate, w_up, w_down, sub):
    bsz, s, d = x.shape
    f = w_gate.shape[1]
    tm = _tile(s, 512)
    tf = _tile(f, 512)
    return pl.pallas_call(
        functools.partial(_ffn_kernel, sub=sub, row_groups=2 if tm % 256 == 0 else 1),
        grid=(bsz, s // tm, f // tf),
        in_specs=[pl.BlockSpec((1, tm, d), lambda b, i, j: (b, i, 0)),
                  pl.BlockSpec((1, N_MOD, d), lambda b, i, j: (b, 0, 0)),
                  pl.BlockSpec((3, d), lambda b, i, j: (0, 0)),
                  pl.BlockSpec((d, tf), lambda b, i, j: (0, j)),
                  pl.BlockSpec((d, tf), lambda b, i, j: (0, j)),
                  pl.BlockSpec((tf, d), lambda b, i, j: (j, 0))],
        out_specs=pl.BlockSpec((1, tm, d), lambda b, i, j: (b, i, 0)),
        out_shape=jax.ShapeDtypeStruct((bsz, s, d), F32),
        scratch_shapes=[pltpu.VMEM((tm, d), BF16)],
        compiler_params=_params("parallel", "parallel", "arbitrary"),
        name="ffn",
    )(x, mod, norm_g, w_gate, w_up, w_down)


def _qk_norm_store(acc, gain, group_mean, o_ref, post_scale):
    width = group_mean.shape[0]
    if post_scale != 1.0:
        gain = gain * post_scale
    for j in range(acc.shape[1] // width):
        sl = slice(j * width, (j + 1) * width)
        y = acc[:, sl]
        y2 = y * y
        y2_hi = y2.astype(BF16)
        y2_lo = (y2 - y2_hi.astype(F32)).astype(BF16)
        ms = (jnp.dot(y2_hi, group_mean, preferred_element_type=F32)
              + jnp.dot(y2_lo, group_mean, preferred_element_type=F32))
        yn = (y * lax.rsqrt(ms + RMS_EPS)) * gain[:, sl]
        o_ref[0, :, sl] = yn.astype(o_ref.dtype)


def _proj_in_kernel(x_ref, mod_ref, g_ref, w_ref, *rest, qk_norm, q_scale, tiles_per_section):
    if qk_norm:
        gains_ref, gmean_ref, o_ref, h_ref = rest
    else:
        o_ref, h_ref = rest
    n = pl.program_id(2)

    @pl.when(n == 0)
    def _():
        _norm_modulate_into(h_ref, x_ref, g_ref[1:2, :], mod_ref[0, 3:4, :], mod_ref[0, 4:5, :])

    acc = jnp.dot(h_ref[...], w_ref[...], preferred_element_type=F32)
    if not qk_norm:
        o_ref[0] = acc.astype(o_ref.dtype)
        return

    section = n // tiles_per_section

    @pl.when(section == 0)
    def _():
        _qk_norm_store(acc, gains_ref[0:1, :], gmean_ref[...], o_ref, q_scale)

    @pl.when(section == 1)
    def _():
        _qk_norm_store(acc, gains_ref[1:2, :], gmean_ref[...], o_ref, 1.0)

    @pl.when(section == 2)
    def _():
        o_ref[0] = acc.astype(o_ref.dtype)


def _proj_in(x, mod, norm_g, w, out_dtype, gains=None, q_scale=1.0):
    bsz, s, d = x.shape
    n = w.shape[1]
    tm = _tile(s, 1024)
    tn = _tile(d, 512)
    qk_norm = gains is not None
    in_specs = [pl.BlockSpec((1, tm, d), lambda b, i, j: (b, i, 0)),
                pl.BlockSpec((1, N_MOD, d), lambda b, i, j: (b, 0, 0)),
                pl.BlockSpec((3, d), lambda b, i, j: (0, 0)),
                pl.BlockSpec((d, tn), lambda b, i, j: (0, j))]
    args = [x, mod, norm_g, w]
    if qk_norm:
        group = gains.shape[1]
        width = _tile(tn, 2 * LANES)
        blk = jnp.arange(width) // group
        gmean = jnp.where(blk[:, None] == blk[None, :], 1.0 / group, 0.0).astype(BF16)
        in_specs += [pl.BlockSpec((2, tn), lambda b, i, j: (0, 0)),
                     pl.BlockSpec((width, width), lambda b, i, j: (0, 0))]
        args += [jnp.tile(gains, (1, tn // group)), gmean]
    return pl.pallas_call(
        functools.partial(_proj_in_kernel, qk_norm=qk_norm, q_scale=q_scale,
                          tiles_per_section=d // tn),
        grid=(bsz, s // tm, n // tn),
        in_specs=in_specs,
        out_specs=pl.BlockSpec((1, tm, tn), lambda b, i, j: (b, i, j)),
        out_shape=jax.ShapeDtypeStruct((bsz, s, n), out_dtype),
        scratch_shapes=[pltpu.VMEM((tm, d), BF16)],
        compiler_params=_params("parallel", "parallel", "arbitrary"),
        name="proj_in",
    )(*args)


def _proj_out_kernel(y_ref, w_ref, x_ref, mod_ref, o_ref):
    o_ref[0] = x_ref[0] + mod_ref[0, 5:6, :] * jnp.dot(
        y_ref[0], w_ref[...], preferred_element_type=F32)


def _proj_out(y, w, x, mod):
    bsz, s, d = x.shape
    k = y.shape[2]
    tm = _tile(s, 1024)
    tn = _tile(d, 512)
    return pl.pallas_call(
        _proj_out_kernel,
        grid=(bsz, s // tm, d // tn),
        in_specs=[pl.BlockSpec((1, tm, k), lambda b, i, j: (b, i, 0)),
                  pl.BlockSpec((k, tn), lambda b, i, j: (0, j)),
                  pl.BlockSpec((1, tm, tn), lambda b, i, j: (b, i, j)),
                  pl.BlockSpec((1, N_MOD, tn), lambda b, i, j: (b, 0, j))],
        out_specs=pl.BlockSpec((1, tm, tn), lambda b, i, j: (b, i, j)),
        out_shape=jax.ShapeDtypeStruct((bsz, s, d), F32),
        compiler_params=_params("parallel", "parallel", "parallel"),
        name="proj_out",
    )(y, w, x, mod)


ACC_ROWS = LANES + BF16_SUBLANES
N_BIAS_COLS = 6
ATTN_TILE = 512


def _attn_kernel(skip_ref, q_ref, k_ref, v_ref, slope_ref, qbias_ref, lam_ref, subg_ref,
                 dist_ref, mask_ref, o_ref, kaug_ref, vt_ref, own_ref, s_ref, mt_ref, m_ref,
                 acc_ref, *, tq, tk, lambda_init):
    h = pl.program_id(1)
    qi = pl.program_id(2)
    n_kv = vt_ref.shape[0]

    @pl.when(qi == 0)
    def _():
        own_ref[...] = slope_ref[0] * dist_ref[...] + mask_ref[...]
        row16 = lax.broadcasted_iota(jnp.int32, (BF16_SUBLANES, tk), 0)
        ones_rows = jnp.where(row16 == 0, 1.0, 0.0).astype(BF16)
        lane = lax.broadcasted_iota(jnp.int32, (tk, LANES), 1)
        row = lax.broadcasted_iota(jnp.int32, (tk, LANES), 0)
        pos = jnp.where(lane < N_BIAS_COLS // 2, row >> 1,
                        jnp.where(lane < N_BIAS_COLS, row & 1, 0)).astype(F32).astype(BF16)
        for c in range(n_kv):
            blk = v_ref[0, c * tk:(c + 1) * tk, :].astype(F32)
            vt_ref[c, 0:LANES, :] = blk.T.astype(BF16)
            vt_ref[c, LANES:ACC_ROWS, :] = ones_rows
            kaug_ref[c, :, 0:LANES] = k_ref[0, c * tk:(c + 1) * tk, :]
            kaug_ref[c, :, LANES:2 * LANES] = pos

    q = q_ref[0]
    lo = lax.broadcasted_iota(jnp.int32, (1, LANES), 1) < LANES // 2
    zero = jnp.zeros_like(q)
    q_bias = jnp.broadcast_to(qbias_ref[0], (tq, LANES)).astype(BF16)
    q_aug = (jnp.concatenate([jnp.where(lo, q, zero), q_bias], axis=1),
             jnp.concatenate([jnp.where(lo, zero, q), q_bias], axis=1))
    slope2 = slope_ref[0]
    acc_ref[...] = jnp.zeros_like(acc_ref)
    m_ref[...] = jnp.full(m_ref.shape, NEG_BIG, F32)

    def stage_a(j, slot):
        kt = kaug_ref[j]
        for mp in range(2):
            s = lax.dot_general(kt, q_aug[mp], NT_DIMS, preferred_element_type=F32)
            s_ref[slot, mp] = s
            mt_ref[slot, mp] = jnp.max(s, axis=0, keepdims=True)

    def accumulate(mp, z, tile_max, shift, vt):
        m_old = m_ref[mp]
        m_new = jnp.maximum(m_old, tile_max + shift)
        alpha = jnp.exp2(m_old - m_new)
        p = jnp.exp2(z - (m_new - shift)).astype(BF16)
        acc_ref[mp] = alpha * acc_ref[mp] + jnp.dot(vt, p, preferred_element_type=F32)
        m_ref[mp] = m_new

    def stage_b(j, slot):
        shift = slope2 * (j * tk - qi * tq).astype(F32)
        vt = vt_ref[j]
        for mp in range(2):
            accumulate(mp, s_ref[slot, mp], mt_ref[slot, mp], shift, vt)

    def stage_b_own(slot):
        vt = vt_ref[qi]
        for mp in range(2):
            z = s_ref[slot, mp] + own_ref[...]
            accumulate(mp, z, jnp.max(z, axis=0, keepdims=True), 0.0, vt)

    j0 = jnp.maximum(qi * tq - skip_ref[h] + 1, 0) // tk
    count = qi - j0
    stage_a(j0, 0)

    def pair(t, carry):
        j = j0 + 2 * t
        stage_a(j + 1, 1)
        stage_b(j, 0)
        stage_a(j + 2, 0)
        stage_b(j + 1, 1)
        return carry

    lax.fori_loop(0, count // 2, pair, 0)

    @pl.when(count % 2 == 0)
    def _():
        stage_b_own(0)

    @pl.when(count % 2 == 1)
    def _():
        stage_a(qi, 1)
        stage_b(qi - 1, 0)
        stage_b_own(1)

    lam_v = lam_ref[...]
    lam = (jnp.exp(jnp.sum(lam_v[0:1] * lam_v[1:2], axis=-1, keepdims=True))
           - jnp.exp(jnp.sum(lam_v[2:3] * lam_v[3:4], axis=-1, keepdims=True))
           + lambda_init)
    o = (acc_ref[0, 0:LANES, :] / acc_ref[0, LANES:LANES + 1, :]
         - lam * (acc_ref[1, 0:LANES, :] / acc_ref[1, LANES:LANES + 1, :]))
    ms = jnp.mean(o * o, axis=0, keepdims=True)
    on = (o * lax.rsqrt(ms + RMS_EPS)).T
    o_ref[0] = ((on * subg_ref[...]) * (1.0 - lambda_init)).astype(o_ref.dtype)


def _attention(qkv, slopes2, logit_bound, lam_vecs, subln_g, lambda_init, n_heads):
    bsz, s, d3 = qkv.shape
    d = d3 // 3
    assert d // n_heads == LANES
    tq = _tile(s, ATTN_TILE)
    tk = tq
    assert tk <= 2 * 256

    r = jnp.arange(tk)[:, None]
    c = jnp.arange(tq)[None, :]
    dist = jnp.minimum(0, 2 * (c - r)).astype(F32)
    mask = jnp.where(r // CHUNK <= c // CHUNK, 0.0, NEG_BIG).astype(F32)

    slope_rows = jnp.broadcast_to(slopes2[:, None, None], (n_heads, 1, tq)).astype(F32)
    hi, mid, lo = (p.astype(F32) for p in _split3(slopes2))
    pieces = jnp.stack([2 * hi, 2 * mid, 2 * lo, hi, mid, lo], axis=-1)
    qbias = jnp.zeros((n_heads, 1, LANES), F32).at[:, 0, :N_BIAS_COLS].set(pieces)
    skip = jnp.ceil((F32_MIN_EXP + F32_MANT_BITS + 2.0 * logit_bound) / slopes2)
    skip = jnp.clip(skip, 1, 2 ** 30).astype(jnp.int32)

    grid_spec = pltpu.PrefetchScalarGridSpec(
        num_scalar_prefetch=1,
        grid=(bsz, n_heads, s // tq),
        in_specs=[pl.BlockSpec((1, tq, LANES), lambda b, h, i, sk: (b, i, h)),
                  pl.BlockSpec((1, s, LANES), lambda b, h, i, sk: (b, 0, n_heads + h)),
                  pl.BlockSpec((1, s, LANES), lambda b, h, i, sk: (b, 0, 2 * n_heads + h)),
                  pl.BlockSpec((1, 1, tq), lambda b, h, i, sk: (h, 0, 0)),
                  pl.BlockSpec((1, 1, LANES), lambda b, h, i, sk: (h, 0, 0)),
                  pl.BlockSpec(lam_vecs.shape, lambda b, h, i, sk: (0, 0)),
                  pl.BlockSpec((1, LANES), lambda b, h, i, sk: (0, 0)),
                  pl.BlockSpec((tk, tq), lambda b, h, i, sk: (0, 0)),
                  pl.BlockSpec((tk, tq), lambda b, h, i, sk: (0, 0))],
        out_specs=pl.BlockSpec((1, tq, LANES), lambda b, h, i, sk: (b, i, h)),
        scratch_shapes=[pltpu.VMEM((s // tk, tk, 2 * LANES), BF16),
                        pltpu.VMEM((s // tk, ACC_ROWS, tk), BF16),
                        pltpu.VMEM((tk, tq), F32),
                        pltpu.VMEM((2, 2, tk, tq), F32),
                        pltpu.VMEM((2, 2, 1, tq), F32),
                        pltpu.VMEM((2, 1, tq), F32),
                        pltpu.VMEM((2, ACC_ROWS, tq), F32)])
    return pl.pallas_call(
        functools.partial(_attn_kernel, tq=tq, tk=tk, lambda_init=lambda_init),
        grid_spec=grid_spec,
        out_shape=jax.ShapeDtypeStruct((bsz, s, d), BF16),
        compiler_params=_params("parallel", "parallel", "arbitrary"),
        name="diff_attn",
    )(skip, qkv, qkv, qkv, slope_rows, qbias, lam_vecs, subln_g.reshape(1, LANES), dist, mask)


def _conv_kernel(b_ref, c_ref, u_ref, ch_ref, uh_ref, w_ref, o_ref, ext_ref):
    i = pl.program_id(1)
    tm = b_ref.shape[1]
    halo = ch_ref[0] * uh_ref[0]
    ext_ref[0:SUBLANES, :] = jnp.where(i == 0, 0.0, halo)
    v = c_ref[0] * u_ref[0]
    ext_ref[SUBLANES:, :] = v
    v1 = ext_ref[SUBLANES - 1:SUBLANES - 1 + tm, :]
    v2 = ext_ref[SUBLANES - 2:SUBLANES - 2 + tm, :]
    w = w_ref[...]
    y = w[0:1] * v2 + w[1:2] * v1 + w[2:3] * v
    o_ref[0] = (b_ref[0] * y).astype(o_ref.dtype)


def _conv_core(bcu, conv_w):
    bsz, s, d3 = bcu.shape
    d = d3 // 3
    tm = _tile(s, 512)
    tc = _tile(d, 512)
    nc = d // tc
    hb = tm // SUBLANES

    def halo_map(off):
        return lambda b, i, j: (b, jnp.maximum(i * hb - 1, 0), off + j)

    return pl.pallas_call(
        _conv_kernel,
        grid=(bsz, s // tm, nc),
        in_specs=[pl.BlockSpec((1, tm, tc), lambda b, i, j: (b, i, j)),
                  pl.BlockSpec((1, tm, tc), lambda b, i, j: (b, i, nc + j)),
                  pl.BlockSpec((1, tm, tc), lambda b, i, j: (b, i, 2 * nc + j)),
                  pl.BlockSpec((1, SUBLANES, tc), halo_map(nc)),
                  pl.BlockSpec((1, SUBLANES, tc), halo_map(2 * nc)),
                  pl.BlockSpec((CONV_WIDTH, tc), lambda b, i, j: (0, j))],
        out_specs=pl.BlockSpec((1, tm, tc), lambda b, i, j: (b, i, j)),
        out_shape=jax.ShapeDtypeStruct((bsz, s, d), BF16),
        scratch_shapes=[pltpu.VMEM((tm + SUBLANES, tc), F32)],
        compiler_params=_params("parallel", "parallel", "parallel"),
        name="short_conv",
    )(bcu, bcu, bcu, bcu, bcu, conv_w)


HGRN_CHUNKS_PER_STEP = 8


def _hgrn_kernel(q_ref, fl_ref, i_ref, g_ref, lbl_ref, ong_ref, o_ref,
                 state_ref, b_ref, *, layer, ts):
    t = pl.program_id(2)
    c64 = CHUNK
    sb = SUBLANES

    @pl.when(t == 0)
    def _():
        state_ref[...] = jnp.zeros_like(state_ref)

    lbl = lbl_ref[...]
    e = jnp.exp(lbl - jnp.max(lbl, axis=0, keepdims=True))
    sm = e / jnp.sum(e, axis=0, keepdims=True)
    lb = jnp.zeros((1, LANES), F32)
    for r in range(1, layer + 1):
        lb = lb + sm[r:r + 1]

    row = lax.broadcasted_iota(jnp.int32, (c64, c64), 0)
    col = lax.broadcasted_iota(jnp.int32, (c64, c64), 1)
    tri = jnp.where(row >= col, 1.0, 0.0).astype(BF16)
    ones = jnp.ones((LANES, LANES), BF16)
    sub_row = lax.broadcasted_iota(jnp.int32, (sb, LANES), 0)
    col8 = lax.broadcasted_iota(jnp.int32, (sb, c64), 1)
    groups = (2 * sb, 4 * sb, 8 * sb)
    level_masks = [(row // grp == col // grp) & (row % grp >= grp // 2) & (col % grp < grp // 2)
                   for grp in groups]


    def gates_and_cumsum(r0, u):
        fl = fl_ref[0, pl.ds(r0, c64), :]
        f = lb + (1.0 - lb) * jax.nn.sigmoid(fl)
        lf = jnp.log(f)
        b = jnp.zeros((c64, LANES), F32)
        for piece in _split3(lf):
            b = b + jnp.dot(tri, piece, preferred_element_type=F32)
        b_ref[u] = b
        return dict(q=q_ref[0, pl.ds(r0, c64), :], kk=1.0 - f, b=b,
                    i16=i_ref[0, pl.ds(r0, c64), :].astype(BF16))

    def diagonal_sums(ch, u):
        w_rows = []
        for blk in range(c64 // sb):
            b_blk = ch["b"][blk * sb:(blk + 1) * sb]
            q_blk = ch["q"][blk * sb:(blk + 1) * sb]
            for s_ in range(sb):
                src = blk * sb + s_
                dec = jnp.exp(jnp.minimum(b_blk - b_ref[u, src:src + 1, :], 0.0))
                w_rows.append(jnp.where(sub_row >= s_,
                                        q_blk * dec * ch["kk"][src:src + 1, :], 0.0))
        w_all = jnp.concatenate(w_rows, axis=0).astype(BF16)
        ch["sums"] = jnp.dot(w_all, ones, preferred_element_type=F32)

    def level_scores(ch, u):
        q, kk, b = ch["q"], ch["kk"], ch["b"]
        out = []
        for grp in groups:
            parts = []
            for gidx in range(c64 // grp):
                edge = gidx * grp + grp // 2 - 1
                parts.append(jnp.broadcast_to(b_ref[u, edge:edge + 1, :], (grp, LANES)))
            ref = parts[0] if len(parts) == 1 else jnp.concatenate(parts, axis=0)
            qd = (q * jnp.exp(jnp.minimum(b - ref, 0.0))).astype(BF16)
            kd = (kk * jnp.exp(jnp.minimum(ref - b, 0.0))).astype(BF16)
            out.append(lax.dot_general(qd, kd, NT_DIMS, preferred_element_type=F32))
        ch["levels"] = out
        b_last = b_ref[u, c64 - 1:c64, :]
        kd_end = (kk * jnp.exp(b_last - b)).astype(BF16)
        ch["upd"] = lax.dot_general(ch["i16"], kd_end, TN_DIMS,
                                    preferred_element_type=F32)
        ch["decay"] = jnp.exp(b_last)
        ch["q_in"] = (q * jnp.exp(b)).astype(BF16)

    def intra_chunk(ch):
        score_rows = []
        for blk in range(c64 // sb):
            acc = jnp.zeros((sb, c64), F32)
            for s_ in range(sb):
                idx = blk * sb + s_
                acc = jnp.where(col8 == idx, ch["sums"][idx * sb:(idx + 1) * sb, :c64], acc)
            score_rows.append(acc)
        scores = jnp.concatenate(score_rows, axis=0)
        for lvl in range(len(groups)):
            scores = scores + jnp.where(level_masks[lvl], ch["levels"][lvl], 0.0)
        ch["intra"] = jnp.dot(scores.astype(BF16), ch["i16"], preferred_element_type=F32)

    def step(c, carry):
        base = c * (HGRN_CHUNKS_PER_STEP * c64)
        starts = [pl.multiple_of(base + u * c64, c64) for u in range(HGRN_CHUNKS_PER_STEP)]
        chunks = [gates_and_cumsum(r0, u) for u, r0 in enumerate(starts)]
        for u, ch in enumerate(chunks):
            diagonal_sums(ch, u)
        for u, ch in enumerate(chunks):
            level_scores(ch, u)
        for ch in chunks:
            intra_chunk(ch)
        st = state_ref[...]
        for r0, ch in zip(starts, chunks):
            inter = lax.dot_general(ch["q_in"], st.astype(BF16), NT_DIMS,
                                    preferred_element_type=F32)
            st = st * ch["decay"] + ch["upd"]
            o = inter + ch["intra"]
            ms = jnp.mean(o * o, axis=-1, keepdims=True)
            on = (o * lax.rsqrt(ms + RMS_EPS)) * ong_ref[...]
            gv = g_ref[0, pl.ds(r0, c64), :]
            o_ref[0, pl.ds(r0, c64), :] = (on * (gv * jax.nn.sigmoid(gv))).astype(o_ref.dtype)
        state_ref[...] = st
        return carry

    lax.fori_loop(0, ts // (HGRN_CHUNKS_PER_STEP * c64), step, 0)


def _hgrn_core(proj, lb_logits, o_norm_g, layer, n_heads):
    bsz, s, d4 = proj.shape
    d = d4 // 4
    assert d // n_heads == LANES
    ts = _tile(s, 512)
    assert ts % (HGRN_CHUNKS_PER_STEP * CHUNK) == 0
    depth = lb_logits.shape[0]
    return pl.pallas_call(
        functools.partial(_hgrn_kernel, layer=layer, ts=ts),
        grid=(bsz, n_heads, s // ts),
        in_specs=[pl.BlockSpec((1, ts, LANES), lambda b, h, t: (b, t, h)),
                  pl.BlockSpec((1, ts, LANES), lambda b, h, t: (b, t, n_heads + h)),
                  pl.BlockSpec((1, ts, LANES), lambda b, h, t: (b, t, 2 * n_heads + h)),
                  pl.BlockSpec((1, ts, LANES), lambda b, h, t: (b, t, 3 * n_heads + h)),
                  pl.BlockSpec((depth, LANES), lambda b, h, t: (0, h)),
                  pl.BlockSpec((1, LANES), lambda b, h, t: (0, 0))],
        out_specs=pl.BlockSpec((1, ts, LANES), lambda b, h, t: (b, t, h)),
        out_shape=jax.ShapeDtypeStruct((bsz, s, d), BF16),
        scratch_shapes=[pltpu.VMEM((LANES, LANES), F32),
                        pltpu.VMEM((HGRN_CHUNKS_PER_STEP, CHUNK, LANES), F32)],
        compiler_params=_params("parallel", "parallel", "arbitrary"),
        name="hgrn2",
    )(proj, proj, proj, proj, lb_logits, o_norm_g.reshape(1, LANES))


def kernel(x, c, ada_w, ada_b, norm_g, ffn_w_gate, ffn_w_up, ffn_w_down,
           attn_w_in, attn_w_out, attn_q_gain, attn_k_gain, attn_lambda, attn_subln_g,
           conv_w_in, conv_w, conv_w_out,
           hgrn_w_in, hgrn_w_out, hgrn_o_norm_g, hgrn_lb_logits):
    depth = ada_w.shape[0]
    d = x.shape[-1]
    qk_dim = attn_q_gain.shape[-1]
    attn_heads = d // (2 * qk_dim)
    hgrn_heads = d // hgrn_o_norm_g.shape[-1]
    bf = lambda w: w.astype(BF16)

    mod = _ada_mod(c, ada_w, ada_b)
    slopes2 = jnp.asarray(
        [2.0 ** (-8.0 * (h + 1) / attn_heads) * LOG2E for h in range(attn_heads)], F32)
    q_scale = qk_dim ** -0.5 * LOG2E

    for layer in range(depth):
        m = mod[layer]
        g = norm_g[layer]
        x = _ffn(x, m, g, bf(ffn_w_gate[layer, 0]), bf(ffn_w_up[layer, 0]),
                 bf(ffn_w_down[layer, 0]), sub=0)
        kind, slot = layer % N_MIXERS, layer // N_MIXERS
        if kind == 0:
            lambda_init = 0.8 - 0.6 * math.exp(-0.3 * layer)
            gains = jnp.stack([attn_q_gain[slot], attn_k_gain[slot]])
            qkv = _proj_in(x, m, g, bf(attn_w_in[slot]), BF16, gains=gains, q_scale=q_scale)
            logit_bound = (1.02 * qk_dim * q_scale * jnp.max(jnp.abs(attn_q_gain[slot]))
                           * jnp.max(jnp.abs(attn_k_gain[slot])))
            y = _attention(qkv, slopes2, logit_bound, attn_lambda[slot], attn_subln_g[slot],
                           lambda_init, attn_heads)
            x = _proj_out(y, bf(attn_w_out[slot]), x, m)
        elif kind == 1:
            bcu = _proj_in(x, m, g, bf(conv_w_in[slot]), F32)
            y = _conv_core(bcu, conv_w[slot])
            x = _proj_out(y, bf(conv_w_out[slot]), x, m)
        else:
            proj = _proj_in(x, m, g, bf(hgrn_w_in[slot]), F32)
            y = _hgrn_core(proj, hgrn_lb_logits, hgrn_o_norm_g[slot], layer, hgrn_heads)
            x = _proj_out(y, bf(hgrn_w_out[slot]), x, m)
        x = _ffn(x, m, g, bf(ffn_w_gate[layer, 1]), bf(ffn_w_up[layer, 1]),
                 bf(ffn_w_down[layer, 1]), sub=2)
    return x
```

```python
import functools
import math

import jax
import jax.numpy as jnp
from jax import lax
from jax.experimental import pallas as pl
from jax.experimental.pallas import tpu as pltpu

F32 = jnp.float32
BF16 = jnp.bfloat16

RMS_EPS = 1e-6
CHUNK = 64
N_MIXERS = 3
N_MOD = 9
CONV_WIDTH = 3
LOG2E = 1.4426950408889634
NEG_BIG = -1e30
F32_MIN_EXP = 127
F32_MANT_BITS = 24

LANES = 128
SUBLANES = 8
BF16_SUBLANES = 16
NORM_ROW_BLOCK = 16
V7X_VMEM_LIMIT_BYTES = 56 * 1024 * 1024

NT_DIMS = (((1,), (1,)), ((), ()))
TN_DIMS = (((0,), (0,)), ((), ()))


def _tile(n, preferred):
    for t in (preferred, 2048, 1024, 512, 256, 128, 64, 32, 16, 8):
        if t <= preferred and n % t == 0:
            return t
    return n


def _params(*sem):
    return pltpu.CompilerParams(dimension_semantics=sem,
                                vmem_limit_bytes=V7X_VMEM_LIMIT_BYTES)


def _norm_modulate_into(h_ref, x_ref, g, shift, scale):
    gain = g * (1.0 + scale)
    rows = h_ref.shape[0]
    blk = min(rows, NORM_ROW_BLOCK)
    for r in range(0, rows, blk):
        x = x_ref[0, r:r + blk, :]
        ms = jnp.mean(x * x, axis=-1, keepdims=True)
        h_ref[r:r + blk, :] = ((x * lax.rsqrt(ms + RMS_EPS)) * gain + shift).astype(BF16)


def _split3(x):
    hi = x.astype(BF16)
    r1 = x - hi.astype(F32)
    mid = r1.astype(BF16)
    lo = (r1 - mid.astype(F32)).astype(BF16)
    return hi, mid, lo


def _ada_kernel(c_ref, w_ref, b_ref, o_ref):
    c = c_ref[...]
    cond = (c * jax.nn.sigmoid(c)).astype(BF16)
    o_ref[0] = jnp.dot(cond, w_ref[0].astype(BF16),
                       preferred_element_type=F32) + b_ref[0]


def _ada_mod(c, ada_w, ada_b):
    depth, d, n = ada_w.shape
    bsz = c.shape[0]
    rows = -(-bsz // SUBLANES) * SUBLANES
    c_pad = jnp.zeros((rows, d), F32).at[:bsz].set(c)
    tn = _tile(n, 1024)
    out = pl.pallas_call(
        _ada_kernel,
        grid=(depth, n // tn),
        in_specs=[pl.BlockSpec((rows, d), lambda l, j: (0, 0)),
                  pl.BlockSpec((1, d, tn), lambda l, j: (l, 0, j)),
                  pl.BlockSpec((1, 1, tn), lambda l, j: (l, 0, j))],
        out_specs=pl.BlockSpec((1, rows, tn), lambda l, j: (l, 0, j)),
        out_shape=jax.ShapeDtypeStruct((depth, rows, n), F32),
        compiler_params=_params("parallel", "parallel"),
        name="ada_mod",
    )(c_pad, ada_w, ada_b.reshape(depth, 1, n))
    return out[:, :bsz].reshape(depth, bsz, N_MOD, d)


def _ffn_kernel(x_ref, xn_ref, mod_ref, modn_ref, g_ref, wg_ref, wu_ref, wd_ref, o_ref, h_ref,
                *, sub, row_groups, n_f):
    t = pl.program_id(0)
    f = pl.program_id(1)
    r = 3 * sub
    slot = t % 2
    g = g_ref[sub:sub + 1, :]

    @pl.when((t == 0) & (f == 0))
    def _():
        _norm_modulate_into(h_ref.at[0], x_ref, g,
                            mod_ref[0, r:r + 1, :], mod_ref[0, r + 1:r + 2, :])

    rows = h_ref.shape[1] // row_groups
    slices = [slice(grp * rows, (grp + 1) * rows) for grp in range(row_groups)]

    def step(first, last):
        gated = []
        for sl in slices:
            h = h_ref[slot, sl, :]
            a = jnp.dot(h, wg_ref[...], preferred_element_type=F32)
            b = jnp.dot(h, wu_ref[...], preferred_element_type=F32)
            gated.append((a * jax.nn.sigmoid(a) * b).astype(BF16))
        for sl, p in zip(slices, gated):
            down = jnp.dot(p, wd_ref[...], preferred_element_type=F32)
            if first:
                o_ref[0, sl, :] = down
            else:
                o_ref[0, sl, :] += down
        if last:
            _norm_modulate_into(h_ref.at[1 - slot], xn_ref, g,
                                modn_ref[0, r:r + 1, :], modn_ref[0, r + 1:r + 2, :])
            o_ref[0] = x_ref[0] + (0.5 * mod_ref[0, r + 2:r + 3, :]) * o_ref[0]

    if n_f == 1:
        step(True, True)
    else:
        pl.when(f == 0)(lambda: step(True, False))
        if n_f > 2:
            pl.when((f > 0) & (f < n_f - 1))(lambda: step(False, False))
        pl.when(f == n_f - 1)(lambda: step(False, True))


def _ffn(x, mod, norm_g, w_gate, w_up, w_down, sub):
    bsz, s, d = x.shape
    f = w_gate.shape[1]
    tm = _tile(s, 512)
    tf = _tile(f, 512)
    per_seq = s // tm
    n_t = bsz * per_seq

    def cur(t, j):
        return (t // per_seq, t % per_seq, 0)

    def nxt(t, j):
        return cur(jnp.minimum(t + 1, n_t - 1), j)

    return pl.pallas_call(
        functools.partial(_ffn_kernel, sub=sub, row_groups=2 if tm % 256 == 0 else 1,
                          n_f=f // tf),
        grid=(n_t, f // tf),
        in_specs=[pl.BlockSpec((1, tm, d), cur),
                  pl.BlockSpec((1, tm, d), nxt),
                  pl.BlockSpec((1, N_MOD, d), lambda t, j: (cur(t, j)[0], 0, 0)),
                  pl.BlockSpec((1, N_MOD, d), lambda t, j: (nxt(t, j)[0], 0, 0)),
                  pl.BlockSpec((3, d), lambda t, j: (0, 0)),
                  pl.BlockSpec((d, tf), lambda t, j: (0, j)),
                  pl.BlockSpec((d, tf), lambda t, j: (0, j)),
                  pl.BlockSpec((tf, d), lambda t, j: (j, 0))],
        out_specs=pl.BlockSpec((1, tm, d), cur),
        out_shape=jax.ShapeDtypeStruct((bsz, s, d), F32),
        scratch_shapes=[pltpu.VMEM((2, tm, d), BF16)],
        compiler_params=_params("arbitrary", "arbitrary"),
        name="ffn",
    )(x, x, mod, mod, norm_g, w_gate, w_up, w_down)


def _qk_norm_store(acc, gain, group_mean, o_ref, post_scale):
    width = group_mean.shape[0]
    if post_scale != 1.0:
        gain = gain * post_scale
    for j in range(acc.shape[1] // width):
        sl = slice(j * width, (j + 1) * width)
        y = acc[:, sl]
        y2 = y * y
        y2_hi = y2.astype(BF16)
        y2_lo = (y2 - y2_hi.astype(F32)).astype(BF16)
        ms = (jnp.dot(y2_hi, group_mean, preferred_element_type=F32)
              + jnp.dot(y2_lo, group_mean, preferred_element_type=F32))
        yn = (y * lax.rsqrt(ms + RMS_EPS)) * gain[:, sl]
        o_ref[0, :, sl] = yn.astype(o_ref.dtype)


def _proj_in_kernel(x_ref, mod_ref, g_ref, w_ref, *rest, qk_norm, q_scale, tiles_per_section):
    if qk_norm:
        gains_ref, gmean_ref, o_ref, h_ref = rest
    else:
        o_ref, h_ref = rest
    n = pl.program_id(2)

    @pl.when(n == 0)
    def _():
        _norm_modulate_into(h_ref, x_ref, g_ref[1:2, :], mod_ref[0, 3:4, :], mod_ref[0, 4:5, :])

    acc = jnp.dot(h_ref[...], w_ref[...], preferred_element_type=F32)
    if not qk_norm:
        o_ref[0] = acc.astype(o_ref.dtype)
        return

    section = n // tiles_per_section

    @pl.when(section == 0)
    def _():
        _qk_norm_store(acc, gains_ref[0:1, :], gmean_ref[...], o_ref, q_scale)

    @pl.when(section == 1)
    def _():
        _qk_norm_store(acc, gains_ref[1:2, :], gmean_ref[...], o_ref, 1.0)

    @pl.when(section == 2)
    def _():
        o_ref[0] = acc.astype(o_ref.dtype)


def _proj_in(x, mod, norm_g, w, out_dtype, gains=None, q_scale=1.0):
    bsz, s, d = x.shape
    n = w.shape[1]
    tm = _tile(s, 1024)
    tn = _tile(d, 512)
    qk_norm = gains is not None
    in_specs = [pl.BlockSpec((1, tm, d), lambda b, i, j: (b, i, 0)),
                pl.BlockSpec((1, N_MOD, d), lambda b, i, j: (b, 0, 0)),
                pl.BlockSpec((3, d), lambda b, i, j: (0, 0)),
                pl.BlockSpec((d, tn), lambda b, i, j: (0, j))]
    args = [x, mod, norm_g, w]
    if qk_norm:
        group = gains.shape[1]
        width = _tile(tn, 2 * LANES)
        blk = jnp.arange(width) // group
        gmean = jnp.where(blk[:, None] == blk[None, :], 1.0 / group, 0.0).astype(BF16)
        in_specs += [pl.BlockSpec((2, tn), lambda b, i, j: (0, 0)),
                     pl.BlockSpec((width, width), lambda b, i, j: (0, 0))]
        args += [jnp.tile(gains, (1, tn // group)), gmean]
    return pl.pallas_call(
        functools.partial(_proj_in_kernel, qk_norm=qk_norm, q_scale=q_scale,
                          tiles_per_section=d // tn),
        grid=(bsz, s // tm, n // tn),
        in_specs=in_specs,
        out_specs=pl.BlockSpec((1, tm, tn), lambda b, i, j: (b, i, j)),
        out_shape=jax.ShapeDtypeStruct((bsz, s, n), out_dtype),
        scratch_shapes=[pltpu.VMEM((tm, d), BF16)],
        compiler_params=_params("parallel", "parallel", "arbitrary"),
        name="proj_in",
    )(*args)


def _proj_out_kernel(y_ref, w_ref, x_ref, mod_ref, o_ref):
    o_ref[0] = x_ref[0] + mod_ref[0, 5:6, :] * jnp.dot(
        y_ref[0], w_ref[...], preferred_element_type=F32)


def _proj_out(y, w, x, mod):
    bsz, s, d = x.shape
    k = y.shape[2]
    tm = _tile(s, 1024)
    tn = _tile(d, 512)
    return pl.pallas_call(
        _proj_out_kernel,
        grid=(bsz, s // tm, d // tn),
        in_specs=[pl.BlockSpec((1, tm, k), lambda b, i, j: (b, i, 0)),
                  pl.BlockSpec((k, tn), lambda b, i, j: (0, j)),
                  pl.BlockSpec((1, tm, tn), lambda b, i, j: (b, i, j)),
                  pl.BlockSpec((1, N_MOD, tn), lambda b, i, j: (b, 0, j))],
        out_specs=pl.BlockSpec((1, tm, tn), lambda b, i, j: (b, i, j)),
        out_shape=jax.ShapeDtypeStruct((bsz, s, d), F32),
        compiler_params=_params("parallel", "parallel", "parallel"),
        name="proj_out",
    )(y, w, x, mod)


ACC_ROWS = LANES + BF16_SUBLANES
N_BIAS_COLS = 6
ATTN_TILE = 512
CARRY_SLOT = 2


def _attn_kernel(skip_ref, q_ref, qn_ref, k_ref, v_ref, slope_ref, qbias_ref, lam_ref, subg_ref,
                 dist_ref, mask_ref, o_ref, kaug_ref, vt_ref, own_ref, s_ref, mt_ref, m_ref,
                 acc_ref, *, tq, tk, lambda_init):
    h = pl.program_id(1)
    qi = pl.program_id(2)
    n_kv = vt_ref.shape[0]

    @pl.when(qi == 0)
    def _():
        own_ref[...] = slope_ref[0] * dist_ref[...] + mask_ref[...]
        row16 = lax.broadcasted_iota(jnp.int32, (BF16_SUBLANES, tk), 0)
        ones_rows = jnp.where(row16 == 0, 1.0, 0.0).astype(BF16)
        lane = lax.broadcasted_iota(jnp.int32, (tk, LANES), 1)
        row = lax.broadcasted_iota(jnp.int32, (tk, LANES), 0)
        pos = jnp.where(lane < N_BIAS_COLS // 2, row >> 1,
                        jnp.where(lane < N_BIAS_COLS, row & 1, 0)).astype(F32).astype(BF16)
        for c in range(n_kv):
            blk = v_ref[0, c * tk:(c + 1) * tk, :].astype(F32)
            vt_ref[c, 0:LANES, :] = blk.T.astype(BF16)
            vt_ref[c, LANES:ACC_ROWS, :] = ones_rows
            kaug_ref[c, :, 0:LANES] = k_ref[0, c * tk:(c + 1) * tk, :]
            kaug_ref[c, :, LANES:2 * LANES] = pos

    lo = lax.broadcasted_iota(jnp.int32, (1, LANES), 1) < LANES // 2
    q_bias = jnp.broadcast_to(qbias_ref[0], (tq, LANES)).astype(BF16)

    def augment(q):
        zero = jnp.zeros_like(q)
        return (jnp.concatenate([jnp.where(lo, q, zero), q_bias], axis=1),
                jnp.concatenate([jnp.where(lo, zero, q), q_bias], axis=1))

    def first_tile(qt):
        return jnp.maximum(qt * tq - skip_ref[h] + 1, 0) // tk

    q_aug = augment(q_ref[0])
    slope2 = slope_ref[0]
    acc_ref[...] = jnp.zeros_like(acc_ref)
    m_ref[...] = jnp.full(m_ref.shape, NEG_BIG, F32)

    def stage_a(j, slot, q_pair=q_aug):
        kt = kaug_ref[j]
        for mp in range(2):
            s = lax.dot_general(kt, q_pair[mp], NT_DIMS, preferred_element_type=F32)
            s_ref[slot, mp] = s
            mt_ref[slot, mp] = jnp.max(s, axis=0, keepdims=True)

    @pl.when(qi == 0)
    def _():
        stage_a(0, CARRY_SLOT)

    def accumulate(mp, z, tile_max, shift, vt):
        m_old = m_ref[mp]
        m_new = jnp.maximum(m_old, tile_max + shift)
        alpha = jnp.exp2(m_old - m_new)
        p = jnp.exp2(z - (m_new - shift)).astype(BF16)
        acc_ref[mp] = alpha * acc_ref[mp] + jnp.dot(vt, p, preferred_element_type=F32)
        m_ref[mp] = m_new

    def stage_b(j, slot):
        shift = slope2 * (j * tk - qi * tq).astype(F32)
        vt = vt_ref[j]
        for mp in range(2):
            accumulate(mp, s_ref[slot, mp], mt_ref[slot, mp], shift, vt)

    def stage_b_own(slot):
        vt = vt_ref[qi]
        for mp in range(2):
            z = s_ref[slot, mp] + own_ref[...]
            accumulate(mp, z, jnp.max(z, axis=0, keepdims=True), 0.0, vt)

    def finish():
        qn = jnp.minimum(qi + 1, pl.num_programs(2) - 1)
        stage_a(first_tile(qn), CARRY_SLOT, augment(qn_ref[0]))
        lam_v = lam_ref[...]
        lam = (jnp.exp(jnp.sum(lam_v[0:1] * lam_v[1:2], axis=-1, keepdims=True))
               - jnp.exp(jnp.sum(lam_v[2:3] * lam_v[3:4], axis=-1, keepdims=True))
               + lambda_init)
        o = (acc_ref[0, 0:LANES, :] / acc_ref[0, LANES:LANES + 1, :]
             - lam * (acc_ref[1, 0:LANES, :] / acc_ref[1, LANES:LANES + 1, :]))
        ms = jnp.mean(o * o, axis=0, keepdims=True)
        on = (o * lax.rsqrt(ms + RMS_EPS)).T
        o_ref[0] = ((on * subg_ref[...]) * (1.0 - lambda_init)).astype(o_ref.dtype)

    j0 = first_tile(qi)
    count = qi - j0
    rest = jnp.maximum(count - 1, 0)

    @pl.when(count == 0)
    def _():
        stage_b_own(CARRY_SLOT)
        finish()

    @pl.when(count > 0)
    def _():
        stage_a(j0 + 1, 0)
        stage_b(j0, CARRY_SLOT)

    def pair(t, carry):
        j = j0 + 1 + 2 * t
        stage_a(j + 1, 1)
        stage_b(j, 0)
        stage_a(j + 2, 0)
        stage_b(j + 1, 1)
        return carry

    lax.fori_loop(0, rest // 2, pair, 0)

    @pl.when((count > 0) & (rest % 2 == 0))
    def _():
        stage_b_own(0)
        finish()

    @pl.when((count > 0) & (rest % 2 == 1))
    def _():
        stage_a(qi, 1)
        stage_b(qi - 1, 0)
        stage_b_own(1)
        finish()


def _attention(qkv, slopes2, logit_bound, lam_vecs, subln_g, lambda_init, n_heads):
    bsz, s, d3 = qkv.shape
    d = d3 // 3
    assert d // n_heads == LANES
    tq = _tile(s, ATTN_TILE)
    tk = tq
    n_q = s // tq
    assert tk <= 2 * 256

    r = jnp.arange(tk)[:, None]
    c = jnp.arange(tq)[None, :]
    dist = jnp.minimum(0, 2 * (c - r)).astype(F32)
    mask = jnp.where(r // CHUNK <= c // CHUNK, 0.0, NEG_BIG).astype(F32)

    slope_rows = jnp.broadcast_to(slopes2[:, None, None], (n_heads, 1, tq)).astype(F32)
    hi, mid, lo = (p.astype(F32) for p in _split3(slopes2))
    pieces = jnp.stack([2 * hi, 2 * mid, 2 * lo, hi, mid, lo], axis=-1)
    qbias = jnp.zeros((n_heads, 1, LANES), F32).at[:, 0, :N_BIAS_COLS].set(pieces)
    skip = jnp.ceil((F32_MIN_EXP + F32_MANT_BITS + 2.0 * logit_bound) / slopes2)
    skip = jnp.clip(skip, 1, 2 ** 30).astype(jnp.int32)

    grid_spec = pltpu.PrefetchScalarGridSpec(
        num_scalar_prefetch=1,
        grid=(bsz, n_heads, s // tq),
        in_specs=[pl.BlockSpec((1, tq, LANES), lambda b, h, i, sk: (b, i, h)),
                  pl.BlockSpec((1, tq, LANES),
                               lambda b, h, i, sk: (b, jnp.minimum(i + 1, n_q - 1), h)),
                  pl.BlockSpec((1, s, LANES), lambda b, h, i, sk: (b, 0, n_heads + h)),
                  pl.BlockSpec((1, s, LANES), lambda b, h, i, sk: (b, 0, 2 * n_heads + h)),
                  pl.BlockSpec((1, 1, tq), lambda b, h, i, sk: (h, 0, 0)),
                  pl.BlockSpec((1, 1, LANES), lambda b, h, i, sk: (h, 0, 0)),
                  pl.BlockSpec(lam_vecs.shape, lambda b, h, i, sk: (0, 0)),
                  pl.BlockSpec((1, LANES), lambda b, h, i, sk: (0, 0)),
                  pl.BlockSpec((tk, tq), lambda b, h, i, sk: (0, 0)),
                  pl.BlockSpec((tk, tq), lambda b, h, i, sk: (0, 0))],
        out_specs=pl.BlockSpec((1, tq, LANES), lambda b, h, i, sk: (b, i, h)),
        scratch_shapes=[pltpu.VMEM((s // tk, tk, 2 * LANES), BF16),
                        pltpu.VMEM((s // tk, ACC_ROWS, tk), BF16),
                        pltpu.VMEM((tk, tq), F32),
                        pltpu.VMEM((3, 2, tk, tq), F32),
                        pltpu.VMEM((3, 2, 1, tq), F32),
                        pltpu.VMEM((2, 1, tq), F32),
                        pltpu.VMEM((2, ACC_ROWS, tq), F32)])
    return pl.pallas_call(
        functools.partial(_attn_kernel, tq=tq, tk=tk, lambda_init=lambda_init),
        grid_spec=grid_spec,
        out_shape=jax.ShapeDtypeStruct((bsz, s, d), BF16),
        compiler_params=_params("arbitrary", "arbitrary", "arbitrary"),
        name="diff_attn",
    )(skip, qkv, qkv, qkv, qkv, slope_rows, qbias, lam_vecs, subln_g.reshape(1, LANES),
      dist, mask)


CONV_HALO = BF16_SUBLANES


def _conv_in_kernel(x_ref, xh_ref, mod_ref, g_ref, wb_ref, wc_ref, wu_ref, cw_ref, o_ref,
                    h_ref, ext_ref):
    i = pl.program_id(1)
    n = pl.program_id(2)
    tm = x_ref.shape[1]
    halo = CONV_HALO

    @pl.when(n == 0)
    def _():
        g, shift, scale = g_ref[1:2, :], mod_ref[0, 3:4, :], mod_ref[0, 4:5, :]
        _norm_modulate_into(h_ref.at[0:halo], xh_ref, g, shift, scale)
        _norm_modulate_into(h_ref.at[halo:halo + tm], x_ref, g, shift, scale)

    h_all = h_ref[...]
    c = jnp.dot(h_all, wc_ref[...], preferred_element_type=F32)
    u = jnp.dot(h_all, wu_ref[...], preferred_element_type=F32)
    b = jnp.dot(h_ref[halo:halo + tm, :], wb_ref[...], preferred_element_type=F32)
    v = c * u
    ext_ref[0:halo, :] = jnp.where(i == 0, 0.0, v[0:halo])
    ext_ref[halo:, :] = v[halo:]
    v1 = ext_ref[halo - 1:halo - 1 + tm, :]
    v2 = ext_ref[halo - 2:halo - 2 + tm, :]
    w = cw_ref[...]
    y = w[0:1] * v2 + w[1:2] * v1 + w[2:3] * v[halo:]
    o_ref[0] = (b * y).astype(o_ref.dtype)


def _conv_in(x, mod, norm_g, w_in, conv_w):
    bsz, s, d = x.shape
    assert CONV_WIDTH - 1 <= CONV_HALO
    tm = _tile(s, 1024)
    tn = _tile(d, 512)
    nc = d // tn
    hb = tm // CONV_HALO
    return pl.pallas_call(
        _conv_in_kernel,
        grid=(bsz, s // tm, nc),
        in_specs=[pl.BlockSpec((1, tm, d), lambda b, i, j: (b, i, 0)),
                  pl.BlockSpec((1, CONV_HALO, d),
                               lambda b, i, j: (b, jnp.maximum(i * hb - 1, 0), 0)),
                  pl.BlockSpec((1, N_MOD, d), lambda b, i, j: (b, 0, 0)),
                  pl.BlockSpec((3, d), lambda b, i, j: (0, 0)),
                  pl.BlockSpec((d, tn), lambda b, i, j: (0, j)),
                  pl.BlockSpec((d, tn), lambda b, i, j: (0, nc + j)),
                  pl.BlockSpec((d, tn), lambda b, i, j: (0, 2 * nc + j)),
                  pl.BlockSpec((CONV_WIDTH, tn), lambda b, i, j: (0, j))],
        out_specs=pl.BlockSpec((1, tm, tn), lambda b, i, j: (b, i, j)),
        out_shape=jax.ShapeDtypeStruct((bsz, s, d), BF16),
        scratch_shapes=[pltpu.VMEM((CONV_HALO + tm, d), BF16),
                        pltpu.VMEM((CONV_HALO + tm, tn), F32)],
        compiler_params=_params("parallel", "parallel", "arbitrary"),
        name="conv_in",
    )(x, x, mod, norm_g, w_in, w_in, w_in, conv_w)


HGRN_CHUNKS_PER_STEP = 8


def _hgrn_kernel(q_ref, fl_ref, i_ref, g_ref, lbl_ref, ong_ref, o_ref,
                 state_ref, b_ref, *, layer, ts):
    t = pl.program_id(2)
    c64 = CHUNK
    sb = SUBLANES

    @pl.when(t == 0)
    def _():
        state_ref[...] = jnp.zeros_like(state_ref)

    lbl = lbl_ref[...]
    e = jnp.exp(lbl - jnp.max(lbl, axis=0, keepdims=True))
    sm = e / jnp.sum(e, axis=0, keepdims=True)
    lb = jnp.zeros((1, LANES), F32)
    for r in range(1, layer + 1):
        lb = lb + sm[r:r + 1]

    row = lax.broadcasted_iota(jnp.int32, (c64, c64), 0)
    col = lax.broadcasted_iota(jnp.int32, (c64, c64), 1)
    tri = jnp.where(row >= col, 1.0, 0.0).astype(BF16)
    ones = jnp.ones((LANES, LANES), BF16)
    sub_row = lax.broadcasted_iota(jnp.int32, (sb, LANES), 0)
    col8 = lax.broadcasted_iota(jnp.int32, (sb, c64), 1)
    groups = (2 * sb, 4 * sb, 8 * sb)
    level_masks = [(row // grp == col // grp) & (row % grp >= grp // 2) & (col % grp < grp // 2)
                   for grp in groups]


    def gates_and_cumsum(r0, u):
        fl = fl_ref[0, pl.ds(r0, c64), :]
        f = lb + (1.0 - lb) * jax.nn.sigmoid(fl)
        lf = jnp.log2(f)
        b = jnp.zeros((c64, LANES), F32)
        for piece in _split3(lf):
            b = b + jnp.dot(tri, piece, preferred_element_type=F32)
        b_ref[u] = b
        return dict(q=q_ref[0, pl.ds(r0, c64), :], kk=1.0 - f, b=b,
                    i16=i_ref[0, pl.ds(r0, c64), :].astype(BF16))

    def diagonal_sums(ch, u):
        w_rows = []
        for blk in range(c64 // sb):
            b_blk = ch["b"][blk * sb:(blk + 1) * sb]
            q_blk = ch["q"][blk * sb:(blk + 1) * sb]
            for s_ in range(sb):
                src = blk * sb + s_
                dec = jnp.exp2(b_blk - b_ref[u, src:src + 1, :])
                w_rows.append(jnp.where(sub_row >= s_,
                                        q_blk * dec * ch["kk"][src:src + 1, :], 0.0))
        w_all = jnp.concatenate(w_rows, axis=0).astype(BF16)
        ch["sums"] = jnp.dot(w_all, ones, preferred_element_type=F32)

    def level_scores(ch, u):
        q, kk, b = ch["q"], ch["kk"], ch["b"]
        out = []
        for grp in groups:
            parts = []
            for gidx in range(c64 // grp):
                edge = gidx * grp + grp // 2 - 1
                parts.append(jnp.broadcast_to(b_ref[u, edge:edge + 1, :], (grp, LANES)))
            ref = parts[0] if len(parts) == 1 else jnp.concatenate(parts, axis=0)
            qd = (q * jnp.exp2(jnp.minimum(b - ref, 0.0))).astype(BF16)
            kd = (kk * jnp.exp2(jnp.minimum(ref - b, 0.0))).astype(BF16)
            out.append(lax.dot_general(qd, kd, NT_DIMS, preferred_element_type=F32))
        ch["levels"] = out
        b_last = b_ref[u, c64 - 1:c64, :]
        kd_end = (kk * jnp.exp2(b_last - b)).astype(BF16)
        ch["upd"] = lax.dot_general(ch["i16"], kd_end, TN_DIMS,
                                    preferred_element_type=F32)
        ch["decay"] = jnp.exp2(b_last)
        ch["q_in"] = (q * jnp.exp2(b)).astype(BF16)

    def intra_chunk(ch):
        score_rows = []
        for blk in range(c64 // sb):
            acc = jnp.zeros((sb, c64), F32)
            for s_ in range(sb):
                idx = blk * sb + s_
                acc = jnp.where(col8 == idx, ch["sums"][idx * sb:(idx + 1) * sb, :c64], acc)
            score_rows.append(acc)
        scores = jnp.concatenate(score_rows, axis=0)
        for lvl in range(len(groups)):
            scores = scores + jnp.where(level_masks[lvl], ch["levels"][lvl], 0.0)
        ch["intra"] = jnp.dot(scores.astype(BF16), ch["i16"], preferred_element_type=F32)

    def step(c, carry):
        base = c * (HGRN_CHUNKS_PER_STEP * c64)
        starts = [pl.multiple_of(base + u * c64, c64) for u in range(HGRN_CHUNKS_PER_STEP)]
        chunks = [gates_and_cumsum(r0, u) for u, r0 in enumerate(starts)]
        for u, ch in enumerate(chunks):
            diagonal_sums(ch, u)
        for u, ch in enumerate(chunks):
            level_scores(ch, u)
        for ch in chunks:
            intra_chunk(ch)
        st = state_ref[...]
        for r0, ch in zip(starts, chunks):
            inter = lax.dot_general(ch["q_in"], st.astype(BF16), NT_DIMS,
                                    preferred_element_type=F32)
            st = st * ch["decay"] + ch["upd"]
            o = inter + ch["intra"]
            ms = jnp.mean(o * o, axis=-1, keepdims=True)
            on = (o * lax.rsqrt(ms + RMS_EPS)) * ong_ref[...]
            gv = g_ref[0, pl.ds(r0, c64), :]
            o_ref[0, pl.ds(r0, c64), :] = (on * (gv * jax.nn.sigmoid(gv))).astype(o_ref.dtype)
        state_ref[...] = st
        return carry

    lax.fori_loop(0, ts // (HGRN_CHUNKS_PER_STEP * c64), step, 0)


def _hgrn_core(proj, lb_logits, o_norm_g, layer, n_heads):
    bsz, s, d4 = proj.shape
    d = d4 // 4
    assert d // n_heads == LANES
    ts = _tile(s, 512)
    assert ts % (HGRN_CHUNKS_PER_STEP * CHUNK) == 0
    depth = lb_logits.shape[0]
    return pl.pallas_call(
        functools.partial(_hgrn_kernel, layer=layer, ts=ts),
        grid=(bsz, n_heads, s // ts),
        in_specs=[pl.BlockSpec((1, ts, LANES), lambda b, h, t: (b, t, h)),
                  pl.BlockSpec((1, ts, LANES), lambda b, h, t: (b, t, n_heads + h)),
                  pl.BlockSpec((1, ts, LANES), lambda b, h, t: (b, t, 2 * n_heads + h)),
                  pl.BlockSpec((1, ts, LANES), lambda b, h, t: (b, t, 3 * n_heads + h)),
                  pl.BlockSpec((depth, LANES), lambda b, h, t: (0, h)),
                  pl.BlockSpec((1, LANES), lambda b, h, t: (0, 0))],
        out_specs=pl.BlockSpec((1, ts, LANES), lambda b, h, t: (b, t, h)),
        out_shape=jax.ShapeDtypeStruct((bsz, s, d), BF16),
        scratch_shapes=[pltpu.VMEM((LANES, LANES), F32),
                        pltpu.VMEM((HGRN_CHUNKS_PER_STEP, CHUNK, LANES), F32)],
        compiler_params=_params("parallel", "parallel", "arbitrary"),
        name="hgrn2",
    )(proj, proj, proj, proj, lb_logits, o_norm_g.reshape(1, LANES))


def kernel(x, c, ada_w, ada_b, norm_g, ffn_w_gate, ffn_w_up, ffn_w_down,
           attn_w_in, attn_w_out, attn_q_gain, attn_k_gain, attn_lambda, attn_subln_g,
           conv_w_in, conv_w, conv_w_out,
           hgrn_w_in, hgrn_w_out, hgrn_o_norm_g, hgrn_lb_logits):
    depth = ada_w.shape[0]
    d = x.shape[-1]
    qk_dim = attn_q_gain.shape[-1]
    attn_heads = d // (2 * qk_dim)
    hgrn_heads = d // hgrn_o_norm_g.shape[-1]
    bf = lambda w: w.astype(BF16)

    mod = _ada_mod(c, ada_w, ada_b)
    slopes2 = jnp.asarray(
        [2.0 ** (-8.0 * (h + 1) / attn_heads) * LOG2E for h in range(attn_heads)], F32)
    q_scale = qk_dim ** -0.5 * LOG2E

    for layer in range(depth):
        m = mod[layer]
        g = norm_g[layer]
        x = _ffn(x, m, g, bf(ffn_w_gate[layer, 0]), bf(ffn_w_up[layer, 0]),
                 bf(ffn_w_down[layer, 0]), sub=0)
        kind, slot = layer % N_MIXERS, layer // N_MIXERS
        if kind == 0:
            lambda_init = 0.8 - 0.6 * math.exp(-0.3 * layer)
            gains = jnp.stack([attn_q_gain[slot], attn_k_gain[slot]])
            qkv = _proj_in(x, m, g, bf(attn_w_in[slot]), BF16, gains=gains, q_scale=q_scale)
            logit_bound = (1.02 * qk_dim * q_scale * jnp.max(jnp.abs(attn_q_gain[slot]))
                           * jnp.max(jnp.abs(attn_k_gain[slot])))
            y = _attention(qkv, slopes2, logit_bound, attn_lambda[slot], attn_subln_g[slot],
                           lambda_init, attn_heads)
            x = _proj_out(y, bf(attn_w_out[slot]), x, m)
        elif kind == 1:
            y = _conv_in(x, m, g, bf(conv_w_in[slot]), conv_w[slot])
            x = _proj_out(y, bf(conv_w_out[slot]), x, m)
        else:
            proj = _proj_in(x, m, g, bf(hgrn_w_in[slot]), F32)
            y = _hgrn_core(proj, hgrn_lb_logits, hgrn_o_norm_g[slot], layer, hgrn_heads)
            x = _proj_out(y, bf(hgrn_w_out[slot]), x, m)
        x = _ffn(x, m, g, bf(ffn_w_gate[layer, 1]), bf(ffn_w_up[layer, 1]),
                 bf(ffn_w_down[layer, 1]), sub=2)
    return x
```

```python
import functools
import math

import jax
import jax.numpy as jnp
from jax import lax
from jax.experimental import pallas as pl
from jax.experimental.pallas import tpu as pltpu

F32 = jnp.float32
BF16 = jnp.bfloat16

RMS_EPS = 1e-6
CHUNK = 64
N_MIXERS = 3
N_MOD = 9
CONV_WIDTH = 3
LOG2E = 1.4426950408889634
NEG_BIG = -1e30
F32_MIN_EXP = 127
F32_MANT_BITS = 24

LANES = 128
SUBLANES = 8
BF16_SUBLANES = 16
NORM_ROW_BLOCK = 16
V7X_VMEM_LIMIT_BYTES = 56 * 1024 * 1024

NT_DIMS = (((1,), (1,)), ((), ()))
TN_DIMS = (((0,), (0,)), ((), ()))


def _tile(n, preferred):
    for t in (preferred, 2048, 1024, 512, 256, 128, 64, 32, 16, 8):
        if t <= preferred and n % t == 0:
            return t
    return n


def _params(*sem):
    return pltpu.CompilerParams(dimension_semantics=sem,
                                vmem_limit_bytes=V7X_VMEM_LIMIT_BYTES)


def _norm_modulate_into(h_ref, x_ref, g, shift, scale):
    gain = g * (1.0 + scale)
    rows = h_ref.shape[0]
    blk = min(rows, NORM_ROW_BLOCK)
    for r in range(0, rows, blk):
        x = x_ref[0, r:r + blk, :]
        ms = jnp.mean(x * x, axis=-1, keepdims=True)
        h_ref[r:r + blk, :] = ((x * lax.rsqrt(ms + RMS_EPS)) * gain + shift).astype(BF16)


def _split3(x):
    hi = x.astype(BF16)
    r1 = x - hi.astype(F32)
    mid = r1.astype(BF16)
    lo = (r1 - mid.astype(F32)).astype(BF16)
    return hi, mid, lo


def _ada_kernel(c_ref, w_ref, b_ref, o_ref):
    c = c_ref[...]
    cond = (c * jax.nn.sigmoid(c)).astype(BF16)
    o_ref[0] = jnp.dot(cond, w_ref[0].astype(BF16),
                       preferred_element_type=F32) + b_ref[0]


def _ada_mod(c, ada_w, ada_b):
    depth, d, n = ada_w.shape
    bsz = c.shape[0]
    rows = -(-bsz // SUBLANES) * SUBLANES
    c_pad = jnp.zeros((rows, d), F32).at[:bsz].set(c)
    tn = _tile(n, 1024)
    out = pl.pallas_call(
        _ada_kernel,
        grid=(depth, n // tn),
        in_specs=[pl.BlockSpec((rows, d), lambda l, j: (0, 0)),
                  pl.BlockSpec((1, d, tn), lambda l, j: (l, 0, j)),
                  pl.BlockSpec((1, 1, tn), lambda l, j: (l, 0, j))],
        out_specs=pl.BlockSpec((1, rows, tn), lambda l, j: (l, 0, j)),
        out_shape=jax.ShapeDtypeStruct((depth, rows, n), F32),
        compiler_params=_params("parallel", "parallel"),
        name="ada_mod",
    )(c_pad, ada_w, ada_b.reshape(depth, 1, n))
    return out[:, :bsz].reshape(depth, bsz, N_MOD, d)


def _ffn_kernel(x_ref, mod_ref, modn_ref, g_ref, wg_ref, wu_ref, wd_ref, o_ref, h_ref, acc_ref,
                *, sub, row_groups, n_f):
    t = pl.program_id(0)
    f = pl.program_id(1)
    r = 3 * sub
    carried = n_f > 1
    slot = t % 2 if carried else 0
    g = g_ref[sub:sub + 1, :]

    own_norm = (f == 0) & (t == 0) if carried else f == 0

    @pl.when(own_norm)
    def _():
        _norm_modulate_into(h_ref.at[slot], x_ref, g,
                            mod_ref[0, r:r + 1, :], mod_ref[0, r + 1:r + 2, :])

    rows = h_ref.shape[1] // row_groups
    slices = [slice(grp * rows, (grp + 1) * rows) for grp in range(row_groups)]

    def step(first, last):
        if first:
            o_ref[0] = x_ref[0]
        gated = []
        for sl in slices:
            h = h_ref[slot, sl, :]
            a = jnp.dot(h, wg_ref[...], preferred_element_type=F32)
            b = jnp.dot(h, wu_ref[...], preferred_element_type=F32)
            gated.append((a * jax.nn.sigmoid(a) * b).astype(BF16))
        for sl, p in zip(slices, gated):
            down = jnp.dot(p, wd_ref[...], preferred_element_type=F32)
            if first:
                acc_ref[sl, :] = down
            else:
                acc_ref[sl, :] += down
        if last:
            if carried:
                _norm_modulate_into(h_ref.at[1 - slot], x_ref, g,
                                    modn_ref[0, r:r + 1, :], modn_ref[0, r + 1:r + 2, :])
            o_ref[0] = o_ref[0] + (0.5 * mod_ref[0, r + 2:r + 3, :]) * acc_ref[...]

    if n_f == 1:
        step(True, True)
    else:
        pl.when(f == 0)(lambda: step(True, False))
        if n_f > 2:
            pl.when((f > 0) & (f < n_f - 1))(lambda: step(False, False))
        pl.when(f == n_f - 1)(lambda: step(False, True))


def _ffn(x, mod, norm_g, w_gate, w_up, w_down, sub):
    bsz, s, d = x.shape
    f = w_gate.shape[1]
    tm = _tile(s, 512)
    tf = _tile(f, 512)
    n_f = f // tf
    rt = _RowTiles(bsz, s, tm, n_f)
    return pl.pallas_call(
        functools.partial(_ffn_kernel, sub=sub, row_groups=2 if tm % 256 == 0 else 1, n_f=n_f),
        grid=(rt.n_t, n_f),
        in_specs=[pl.BlockSpec((1, tm, d), rt.x_index),
                  pl.BlockSpec((1, N_MOD, d), rt.mod_index),
                  pl.BlockSpec((1, N_MOD, d), rt.next_mod_index),
                  pl.BlockSpec((3, d), lambda t, j: (0, 0)),
                  pl.BlockSpec((d, tf), lambda t, j: (0, j)),
                  pl.BlockSpec((d, tf), lambda t, j: (0, j)),
                  pl.BlockSpec((tf, d), lambda t, j: (j, 0))],
        out_specs=pl.BlockSpec((1, tm, d), lambda t, j: rt.tile(t) + (0,)),
        out_shape=jax.ShapeDtypeStruct((bsz, s, d), F32),
        scratch_shapes=[pltpu.VMEM((2 if n_f > 1 else 1, tm, d), BF16),
                        pltpu.VMEM((tm, d), F32)],
        compiler_params=_params("arbitrary", "arbitrary"),
        name="ffn",
    )(x, mod, mod, norm_g, w_gate, w_up, w_down)


def _qk_norm_store(acc, gain, group_mean, o_ref, post_scale):
    width = group_mean.shape[0]
    if post_scale != 1.0:
        gain = gain * post_scale
    for j in range(acc.shape[1] // width):
        sl = slice(j * width, (j + 1) * width)
        y = acc[:, sl]
        y2 = y * y
        y2_hi = y2.astype(BF16)
        y2_lo = (y2 - y2_hi.astype(F32)).astype(BF16)
        ms = (jnp.dot(y2_hi, group_mean, preferred_element_type=F32)
              + jnp.dot(y2_lo, group_mean, preferred_element_type=F32))
        yn = (y * lax.rsqrt(ms + RMS_EPS)) * gain[:, sl]
        o_ref[0, :, sl] = yn.astype(o_ref.dtype)


class _RowTiles:
    def __init__(self, bsz, s, tm, n_steps):
        self.per_seq = s // tm
        self.n_t = bsz * self.per_seq
        self.advance_at = max(n_steps // 2, 1)

    def tile(self, t):
        return (t // self.per_seq, t % self.per_seq)

    def nxt(self, t):
        return jnp.minimum(t + 1, self.n_t - 1)

    def x_index(self, t, j):
        return self.tile(jnp.where(j >= self.advance_at, self.nxt(t), t)) + (0,)

    def mod_index(self, t, j):
        return (t // self.per_seq, 0, 0)

    def next_mod_index(self, t, j):
        return (self.nxt(t) // self.per_seq, 0, 0)


def _proj_in_kernel(x_ref, mod_ref, modn_ref, g_ref, w_ref, *rest, qk_norm, q_scale,
                    tiles_per_section, n_n):
    if qk_norm:
        gains_ref, gmean_ref, o_ref, h_ref = rest
    else:
        o_ref, h_ref = rest
    t = pl.program_id(0)
    n = pl.program_id(1)
    carried = n_n > 1
    slot = t % 2 if carried else 0
    g = g_ref[1:2, :]

    @pl.when((n == 0) & (t == 0) if carried else n == 0)
    def _():
        _norm_modulate_into(h_ref.at[slot], x_ref, g, mod_ref[0, 3:4, :], mod_ref[0, 4:5, :])

    def body(last):
        acc = jnp.dot(h_ref[slot], w_ref[...], preferred_element_type=F32)
        if not qk_norm or last:
            o_ref[0] = acc.astype(o_ref.dtype)
        else:
            section = n // tiles_per_section

            @pl.when(section == 0)
            def _():
                _qk_norm_store(acc, gains_ref[0:1, :], gmean_ref[...], o_ref, q_scale)

            @pl.when(section == 1)
            def _():
                _qk_norm_store(acc, gains_ref[1:2, :], gmean_ref[...], o_ref, 1.0)

            @pl.when(section == 2)
            def _():
                o_ref[0] = acc.astype(o_ref.dtype)
        if last and carried:
            _norm_modulate_into(h_ref.at[1 - slot], x_ref, g,
                                modn_ref[0, 3:4, :], modn_ref[0, 4:5, :])

    if n_n == 1:
        body(True)
    else:
        pl.when(n < n_n - 1)(lambda: body(False))
        pl.when(n == n_n - 1)(lambda: body(True))


def _proj_in(x, mod, norm_g, w, out_dtype, gains=None, q_scale=1.0):
    bsz, s, d = x.shape
    n = w.shape[1]
    tm = _tile(s, 1024)
    tn = _tile(d, 512)
    n_n = n // tn
    qk_norm = gains is not None
    assert not qk_norm or (n == 3 * d and n_n > 1)
    rt = _RowTiles(bsz, s, tm, n_n)
    in_specs = [pl.BlockSpec((1, tm, d), rt.x_index),
                pl.BlockSpec((1, N_MOD, d), rt.mod_index),
                pl.BlockSpec((1, N_MOD, d), rt.next_mod_index),
                pl.BlockSpec((3, d), lambda t, j: (0, 0)),
                pl.BlockSpec((d, tn), lambda t, j: (0, j))]
    args = [x, mod, mod, norm_g, w]
    if qk_norm:
        group = gains.shape[1]
        width = _tile(tn, 2 * LANES)
        blk = jnp.arange(width) // group
        gmean = jnp.where(blk[:, None] == blk[None, :], 1.0 / group, 0.0).astype(BF16)
        in_specs += [pl.BlockSpec((2, tn), lambda t, j: (0, 0)),
                     pl.BlockSpec((width, width), lambda t, j: (0, 0))]
        args += [jnp.tile(gains, (1, tn // group)), gmean]
    return pl.pallas_call(
        functools.partial(_proj_in_kernel, qk_norm=qk_norm, q_scale=q_scale,
                          tiles_per_section=d // tn, n_n=n_n),
        grid=(rt.n_t, n_n),
        in_specs=in_specs,
        out_specs=pl.BlockSpec((1, tm, tn), lambda t, j: rt.tile(t) + (j,)),
        out_shape=jax.ShapeDtypeStruct((bsz, s, n), out_dtype),
        scratch_shapes=[pltpu.VMEM((2 if n_n > 1 else 1, tm, d), BF16)],
        compiler_params=_params("arbitrary", "arbitrary"),
        name="proj_in",
    )(*args)


def _proj_out_kernel(y_ref, w_ref, x_ref, mod_ref, o_ref):
    o_ref[0] = x_ref[0] + mod_ref[0, 5:6, :] * jnp.dot(
        y_ref[0], w_ref[...], preferred_element_type=F32)


def _proj_out(y, w, x, mod):
    bsz, s, d = x.shape
    k = y.shape[2]
    tm = _tile(s, 1024)
    tn = _tile(d, 512)
    return pl.pallas_call(
        _proj_out_kernel,
        grid=(bsz, s // tm, d // tn),
        in_specs=[pl.BlockSpec((1, tm, k), lambda b, i, j: (b, i, 0)),
                  pl.BlockSpec((k, tn), lambda b, i, j: (0, j)),
                  pl.BlockSpec((1, tm, tn), lambda b, i, j: (b, i, j)),
                  pl.BlockSpec((1, N_MOD, tn), lambda b, i, j: (b, 0, j))],
        out_specs=pl.BlockSpec((1, tm, tn), lambda b, i, j: (b, i, j)),
        out_shape=jax.ShapeDtypeStruct((bsz, s, d), F32),
        compiler_params=_params("parallel", "parallel", "parallel"),
        name="proj_out",
    )(y, w, x, mod)


ACC_ROWS = LANES + BF16_SUBLANES
N_BIAS_COLS = 6
ATTN_TILE = 512
CARRY_SLOT = 2


def _attn_kernel(skip_ref, q_ref, qn_ref, k_ref, v_ref, slope_ref, qbias_ref, lam_ref, subg_ref,
                 dist_ref, mask_ref, o_ref, kaug_ref, vt_ref, own_ref, s_ref, mt_ref, m_ref,
                 acc_ref, *, tq, tk, lambda_init):
    h = pl.program_id(1)
    qi = pl.program_id(2)
    n_kv = vt_ref.shape[0]

    @pl.when(qi == 0)
    def _():
        own_ref[...] = slope_ref[0] * dist_ref[...] + mask_ref[...]
        row16 = lax.broadcasted_iota(jnp.int32, (BF16_SUBLANES, tk), 0)
        ones_rows = jnp.where(row16 == 0, 1.0, 0.0).astype(BF16)
        lane = lax.broadcasted_iota(jnp.int32, (tk, LANES), 1)
        row = lax.broadcasted_iota(jnp.int32, (tk, LANES), 0)
        pos = jnp.where(lane < N_BIAS_COLS // 2, row >> 1,
                        jnp.where(lane < N_BIAS_COLS, row & 1, 0)).astype(F32).astype(BF16)
        for c in range(n_kv):
            blk = v_ref[0, c * tk:(c + 1) * tk, :].astype(F32)
            vt_ref[c, 0:LANES, :] = blk.T.astype(BF16)
            vt_ref[c, LANES:ACC_ROWS, :] = ones_rows
            kaug_ref[c, :, 0:LANES] = k_ref[0, c * tk:(c + 1) * tk, :]
            kaug_ref[c, :, LANES:2 * LANES] = pos

    lo = lax.broadcasted_iota(jnp.int32, (1, LANES), 1) < LANES // 2
    q_bias = jnp.broadcast_to(qbias_ref[0], (tq, LANES)).astype(BF16)

    def augment(q):
        zero = jnp.zeros_like(q)
        return (jnp.concatenate([jnp.where(lo, q, zero), q_bias], axis=1),
                jnp.concatenate([jnp.where(lo, zero, q), q_bias], axis=1))

    def first_tile(qt):
        return jnp.maximum(qt * tq - skip_ref[h] + 1, 0) // tk

    q_aug = augment(q_ref[0])
    slope2 = slope_ref[0]
    acc_ref[...] = jnp.zeros_like(acc_ref)
    m_ref[...] = jnp.full(m_ref.shape, NEG_BIG, F32)

    def stage_a(j, slot, q_pair=q_aug):
        kt = kaug_ref[j]
        for mp in range(2):
            s = lax.dot_general(kt, q_pair[mp], NT_DIMS, preferred_element_type=F32)
            s_ref[slot, mp] = s
            mt_ref[slot, mp] = jnp.max(s, axis=0, keepdims=True)

    @pl.when(qi == 0)
    def _():
        stage_a(0, CARRY_SLOT)

    def accumulate(mp, z, tile_max, shift, vt):
        m_old = m_ref[mp]
        m_new = jnp.maximum(m_old, tile_max + shift)
        alpha = jnp.exp2(m_old - m_new)
        p = jnp.exp2(z - (m_new - shift)).astype(BF16)
        acc_ref[mp] = alpha * acc_ref[mp] + jnp.dot(vt, p, preferred_element_type=F32)
        m_ref[mp] = m_new

    def stage_b(j, slot):
        shift = slope2 * (j * tk - qi * tq).astype(F32)
        vt = vt_ref[j]
        for mp in range(2):
            accumulate(mp, s_ref[slot, mp], mt_ref[slot, mp], shift, vt)

    def stage_b_own(slot):
        vt = vt_ref[qi]
        for mp in range(2):
            z = s_ref[slot, mp] + own_ref[...]
            accumulate(mp, z, jnp.max(z, axis=0, keepdims=True), 0.0, vt)

    def finish():
        qn = jnp.minimum(qi + 1, pl.num_programs(2) - 1)
        stage_a(first_tile(qn), CARRY_SLOT, augment(qn_ref[0]))
        lam_v = lam_ref[...]
        lam = (jnp.exp(jnp.sum(lam_v[0:1] * lam_v[1:2], axis=-1, keepdims=True))
               - jnp.exp(jnp.sum(lam_v[2:3] * lam_v[3:4], axis=-1, keepdims=True))
               + lambda_init)
        o = (acc_ref[0, 0:LANES, :] / acc_ref[0, LANES:LANES + 1, :]
             - lam * (acc_ref[1, 0:LANES, :] / acc_ref[1, LANES:LANES + 1, :]))
        ms = jnp.mean(o * o, axis=0, keepdims=True)
        on = (o * lax.rsqrt(ms + RMS_EPS)).T
        o_ref[0] = ((on * subg_ref[...]) * (1.0 - lambda_init)).astype(o_ref.dtype)

    j0 = first_tile(qi)
    count = qi - j0
    rest = jnp.maximum(count - 1, 0)

    @pl.when(count == 0)
    def _():
        stage_b_own(CARRY_SLOT)
        finish()

    @pl.when(count > 0)
    def _():
        stage_a(j0 + 1, 0)
        stage_b(j0, CARRY_SLOT)

    def pair(t, carry):
        j = j0 + 1 + 2 * t
        stage_a(j + 1, 1)
        stage_b(j, 0)
        stage_a(j + 2, 0)
        stage_b(j + 1, 1)
        return carry

    lax.fori_loop(0, rest // 2, pair, 0)

    @pl.when((count > 0) & (rest % 2 == 0))
    def _():
        stage_b_own(0)
        finish()

    @pl.when((count > 0) & (rest % 2 == 1))
    def _():
        stage_a(qi, 1)
        stage_b(qi - 1, 0)
        stage_b_own(1)
        finish()


def _attention(qkv, slopes2, logit_bound, lam_vecs, subln_g, lambda_init, n_heads):
    bsz, s, d3 = qkv.shape
    d = d3 // 3
    assert d // n_heads == LANES
    tq = _tile(s, ATTN_TILE)
    tk = tq
    n_q = s // tq
    assert tk <= 2 * 256

    r = jnp.arange(tk)[:, None]
    c = jnp.arange(tq)[None, :]
    dist = jnp.minimum(0, 2 * (c - r)).astype(F32)
    mask = jnp.where(r // CHUNK <= c // CHUNK, 0.0, NEG_BIG).astype(F32)

    slope_rows = jnp.broadcast_to(slopes2[:, None, None], (n_heads, 1, tq)).astype(F32)
    hi, mid, lo = (p.astype(F32) for p in _split3(slopes2))
    pieces = jnp.stack([2 * hi, 2 * mid, 2 * lo, hi, mid, lo], axis=-1)
    qbias = jnp.zeros((n_heads, 1, LANES), F32).at[:, 0, :N_BIAS_COLS].set(pieces)
    skip = jnp.ceil((F32_MIN_EXP + F32_MANT_BITS + 2.0 * logit_bound) / slopes2)
    skip = jnp.clip(skip, 1, 2 ** 30).astype(jnp.int32)

    grid_spec = pltpu.PrefetchScalarGridSpec(
        num_scalar_prefetch=1,
        grid=(bsz, n_heads, s // tq),
        in_specs=[pl.BlockSpec((1, tq, LANES), lambda b, h, i, sk: (b, i, h)),
                  pl.BlockSpec((1, tq, LANES),
                               lambda b, h, i, sk: (b, jnp.minimum(i + 1, n_q - 1), h)),
                  pl.BlockSpec((1, s, LANES), lambda b, h, i, sk: (b, 0, n_heads + h)),
                  pl.BlockSpec((1, s, LANES), lambda b, h, i, sk: (b, 0, 2 * n_heads + h)),
                  pl.BlockSpec((1, 1, tq), lambda b, h, i, sk: (h, 0, 0)),
                  pl.BlockSpec((1, 1, LANES), lambda b, h, i, sk: (h, 0, 0)),
                  pl.BlockSpec(lam_vecs.shape, lambda b, h, i, sk: (0, 0)),
                  pl.BlockSpec((1, LANES), lambda b, h, i, sk: (0, 0)),
                  pl.BlockSpec((tk, tq), lambda b, h, i, sk: (0, 0)),
                  pl.BlockSpec((tk, tq), lambda b, h, i, sk: (0, 0))],
        out_specs=pl.BlockSpec((1, tq, LANES), lambda b, h, i, sk: (b, i, h)),
        scratch_shapes=[pltpu.VMEM((s // tk, tk, 2 * LANES), BF16),
                        pltpu.VMEM((s // tk, ACC_ROWS, tk), BF16),
                        pltpu.VMEM((tk, tq), F32),
                        pltpu.VMEM((3, 2, tk, tq), F32),
                        pltpu.VMEM((3, 2, 1, tq), F32),
                        pltpu.VMEM((2, 1, tq), F32),
                        pltpu.VMEM((2, ACC_ROWS, tq), F32)])
    return pl.pallas_call(
        functools.partial(_attn_kernel, tq=tq, tk=tk, lambda_init=lambda_init),
        grid_spec=grid_spec,
        out_shape=jax.ShapeDtypeStruct((bsz, s, d), BF16),
        compiler_params=_params("arbitrary", "arbitrary", "arbitrary"),
        name="diff_attn",
    )(skip, qkv, qkv, qkv, qkv, slope_rows, qbias, lam_vecs, subln_g.reshape(1, LANES),
      dist, mask)


CONV_HALO = BF16_SUBLANES


def _conv_in_kernel(x_ref, mod_ref, modn_ref, g_ref, wb_ref, wc_ref, wu_ref, cw_ref, o_ref,
                    h_ref, ext_ref, *, per_seq, n_n):
    t = pl.program_id(0)
    n = pl.program_id(1)
    tm = x_ref.shape[1]
    halo = CONV_HALO
    carried = n_n > 1
    slot = t % 2 if carried else 0
    g = g_ref[1:2, :]

    def fill(dst, src, mod_blk):
        h_ref[dst, 0:halo, :] = h_ref[src, tm:tm + halo, :]
        _norm_modulate_into(h_ref.at[dst].at[halo:halo + tm], x_ref, g,
                            mod_blk[0, 3:4, :], mod_blk[0, 4:5, :])

    @pl.when((n == 0) & (t == 0))
    def _():
        h_ref[slot, tm:tm + halo, :] = jnp.zeros((halo, h_ref.shape[2]), BF16)

    @pl.when((n == 0) & (t == 0) if carried else n == 0)
    def _():
        fill(slot, slot, mod_ref)

    def body(last):
        h_all = h_ref[slot]
        c = jnp.dot(h_all, wc_ref[...], preferred_element_type=F32)
        u = jnp.dot(h_all, wu_ref[...], preferred_element_type=F32)
        b = jnp.dot(h_ref[slot, halo:halo + tm, :], wb_ref[...], preferred_element_type=F32)
        v = c * u
        ext_ref[0:halo, :] = jnp.where(t % per_seq == 0, 0.0, v[0:halo])
        ext_ref[halo:, :] = v[halo:]
        v1 = ext_ref[halo - 1:halo - 1 + tm, :]
        v2 = ext_ref[halo - 2:halo - 2 + tm, :]
        w = cw_ref[...]
        y = w[0:1] * v2 + w[1:2] * v1 + w[2:3] * v[halo:]
        o_ref[0] = (b * y).astype(o_ref.dtype)
        if last and carried:
            fill(1 - slot, slot, modn_ref)

    if n_n == 1:
        body(True)
    else:
        pl.when(n < n_n - 1)(lambda: body(False))
        pl.when(n == n_n - 1)(lambda: body(True))


def _conv_in(x, mod, norm_g, w_in, conv_w):
    bsz, s, d = x.shape
    assert CONV_WIDTH - 1 <= CONV_HALO
    tm = _tile(s, 1024)
    tn = _tile(d, 512)
    nc = d // tn
    rt = _RowTiles(bsz, s, tm, nc)
    return pl.pallas_call(
        functools.partial(_conv_in_kernel, per_seq=rt.per_seq, n_n=nc),
        grid=(rt.n_t, nc),
        in_specs=[pl.BlockSpec((1, tm, d), rt.x_index),
                  pl.BlockSpec((1, N_MOD, d), rt.mod_index),
                  pl.BlockSpec((1, N_MOD, d), rt.next_mod_index),
                  pl.BlockSpec((3, d), lambda t, j: (0, 0)),
                  pl.BlockSpec((d, tn), lambda t, j: (0, j)),
                  pl.BlockSpec((d, tn), lambda t, j: (0, nc + j)),
                  pl.BlockSpec((d, tn), lambda t, j: (0, 2 * nc + j)),
                  pl.BlockSpec((CONV_WIDTH, tn), lambda t, j: (0, j))],
        out_specs=pl.BlockSpec((1, tm, tn), lambda t, j: rt.tile(t) + (j,)),
        out_shape=jax.ShapeDtypeStruct((bsz, s, d), BF16),
        scratch_shapes=[pltpu.VMEM((2 if nc > 1 else 1, CONV_HALO + tm, d), BF16),
                        pltpu.VMEM((CONV_HALO + tm, tn), F32)],
        compiler_params=_params("arbitrary", "arbitrary"),
        name="conv_in",
    )(x, mod, mod, norm_g, w_in, w_in, w_in, conv_w)


HGRN_CHUNKS_PER_STEP = 8


def _hgrn_kernel(q_ref, fl_ref, i_ref, g_ref, lbl_ref, ong_ref, o_ref,
                 state_ref, b_ref, *, layer, ts):
    t = pl.program_id(2)
    c64 = CHUNK
    sb = SUBLANES

    @pl.when(t == 0)
    def _():
        state_ref[...] = jnp.zeros_like(state_ref)

    lbl = lbl_ref[...]
    e = jnp.exp(lbl - jnp.max(lbl, axis=0, keepdims=True))
    sm = e / jnp.sum(e, axis=0, keepdims=True)
    lb = jnp.zeros((1, LANES), F32)
    for r in range(1, layer + 1):
        lb = lb + sm[r:r + 1]

    row = lax.broadcasted_iota(jnp.int32, (c64, c64), 0)
    col = lax.broadcasted_iota(jnp.int32, (c64, c64), 1)
    tri = jnp.where(row >= col, 1.0, 0.0).astype(BF16)
    ones = jnp.ones((LANES, LANES), BF16)
    sub_row = lax.broadcasted_iota(jnp.int32, (sb, LANES), 0)
    col8 = lax.broadcasted_iota(jnp.int32, (sb, c64), 1)
    groups = (2 * sb, 4 * sb, 8 * sb)
    level_masks = [(row // grp == col // grp) & (row % grp >= grp // 2) & (col % grp < grp // 2)
                   for grp in groups]


    def gates_and_cumsum(r0, u):
        fl = fl_ref[0, pl.ds(r0, c64), :]
        f = lb + (1.0 - lb) * jax.nn.sigmoid(fl)
        lf = jnp.log2(f)
        b = jnp.zeros((c64, LANES), F32)
        for piece in _split3(lf):
            b = b + jnp.dot(tri, piece, preferred_element_type=F32)
        b_ref[u] = b
        return dict(q=q_ref[0, pl.ds(r0, c64), :], kk=1.0 - f, b=b,
                    i16=i_ref[0, pl.ds(r0, c64), :].astype(BF16))

    def diagonal_sums(ch, u):
        w_rows = []
        for blk in range(c64 // sb):
            b_blk = ch["b"][blk * sb:(blk + 1) * sb]
            q_blk = ch["q"][blk * sb:(blk + 1) * sb]
            for s_ in range(sb):
                src = blk * sb + s_
                dec = jnp.exp2(b_blk - b_ref[u, src:src + 1, :])
                w_rows.append(jnp.where(sub_row >= s_,
                                        q_blk * dec * ch["kk"][src:src + 1, :], 0.0))
        w_all = jnp.concatenate(w_rows, axis=0).astype(BF16)
        ch["sums"] = jnp.dot(w_all, ones, preferred_element_type=F32)

    def level_scores(ch, u):
        q, kk, b = ch["q"], ch["kk"], ch["b"]
        out = []
        for grp in groups:
            parts = []
            for gidx in range(c64 // grp):
                edge = gidx * grp + grp // 2 - 1
                parts.append(jnp.broadcast_to(b_ref[u, edge:edge + 1, :], (grp, LANES)))
            ref = parts[0] if len(parts) == 1 else jnp.concatenate(parts, axis=0)
            qd = (q * jnp.exp2(jnp.minimum(b - ref, 0.0))).astype(BF16)
            kd = (kk * jnp.exp2(jnp.minimum(ref - b, 0.0))).astype(BF16)
            out.append(lax.dot_general(qd, kd, NT_DIMS, preferred_element_type=F32))
        ch["levels"] = out
        b_last = b_ref[u, c64 - 1:c64, :]
        kd_end = (kk * jnp.exp2(b_last - b)).astype(BF16)
        ch["upd"] = lax.dot_general(ch["i16"], kd_end, TN_DIMS,
                                    preferred_element_type=F32)
        ch["decay"] = jnp.exp2(b_last)
        ch["q_in"] = (q * jnp.exp2(b)).astype(BF16)

    def intra_chunk(ch):
        score_rows = []
        for blk in range(c64 // sb):
            acc = jnp.zeros((sb, c64), F32)
            for s_ in range(sb):
                idx = blk * sb + s_
                acc = jnp.where(col8 == idx, ch["sums"][idx * sb:(idx + 1) * sb, :c64], acc)
            score_rows.append(acc)
        scores = jnp.concatenate(score_rows, axis=0)
        for lvl in range(len(groups)):
            scores = scores + jnp.where(level_masks[lvl], ch["levels"][lvl], 0.0)
        ch["intra"] = jnp.dot(scores.astype(BF16), ch["i16"], preferred_element_type=F32)

    per_step = b_ref.shape[0]

    def step(c, carry):
        base = c * (per_step * c64)
        starts = [pl.multiple_of(base + u * c64, c64) for u in range(per_step)]
        chunks = [gates_and_cumsum(r0, u) for u, r0 in enumerate(starts)]
        for u, ch in enumerate(chunks):
            diagonal_sums(ch, u)
        for u, ch in enumerate(chunks):
            level_scores(ch, u)
        for ch in chunks:
            intra_chunk(ch)
        st = state_ref[...]
        for r0, ch in zip(starts, chunks):
            inter = lax.dot_general(ch["q_in"], st.astype(BF16), NT_DIMS,
                                    preferred_element_type=F32)
            st = st * ch["decay"] + ch["upd"]
            o = inter + ch["intra"]
            ms = jnp.mean(o * o, axis=-1, keepdims=True)
            on = (o * lax.rsqrt(ms + RMS_EPS)) * ong_ref[...]
            gv = g_ref[0, pl.ds(r0, c64), :]
            o_ref[0, pl.ds(r0, c64), :] = (on * (gv * jax.nn.sigmoid(gv))).astype(o_ref.dtype)
        state_ref[...] = st
        return carry

    lax.fori_loop(0, ts // (per_step * c64), step, 0)


def _hgrn_core(proj, lb_logits, o_norm_g, layer, n_heads):
    bsz, s, d4 = proj.shape
    d = d4 // 4
    assert d // n_heads == LANES
    ts = _tile(s, 512)
    per_step = math.gcd(HGRN_CHUNKS_PER_STEP, ts // CHUNK)
    depth = lb_logits.shape[0]
    return pl.pallas_call(
        functools.partial(_hgrn_kernel, layer=layer, ts=ts),
        grid=(bsz, n_heads, s // ts),
        in_specs=[pl.BlockSpec((1, ts, LANES), lambda b, h, t: (b, t, h)),
                  pl.BlockSpec((1, ts, LANES), lambda b, h, t: (b, t, n_heads + h)),
                  pl.BlockSpec((1, ts, LANES), lambda b, h, t: (b, t, 2 * n_heads + h)),
                  pl.BlockSpec((1, ts, LANES), lambda b, h, t: (b, t, 3 * n_heads + h)),
                  pl.BlockSpec((depth, LANES), lambda b, h, t: (0, h)),
                  pl.BlockSpec((1, LANES), lambda b, h, t: (0, 0))],
        out_specs=pl.BlockSpec((1, ts, LANES), lambda b, h, t: (b, t, h)),
        out_shape=jax.ShapeDtypeStruct((bsz, s, d), BF16),
        scratch_shapes=[pltpu.VMEM((LANES, LANES), F32),
                        pltpu.VMEM((per_step, CHUNK, LANES), F32)],
        compiler_params=_params("parallel", "parallel", "arbitrary"),
        name="hgrn2",
    )(proj, proj, proj, proj, lb_logits, o_norm_g.reshape(1, LANES))


def kernel(x, c, ada_w, ada_b, norm_g, ffn_w_gate, ffn_w_up, ffn_w_down,
           attn_w_in, attn_w_out, attn_q_gain, attn_k_gain, attn_lambda, attn_subln_g,
           conv_w_in, conv_w, conv_w_out,
           hgrn_w_in, hgrn_w_out, hgrn_o_norm_g, hgrn_lb_logits):
    depth = ada_w.shape[0]
    d = x.shape[-1]
    qk_dim = attn_q_gain.shape[-1]
    attn_heads = d // (2 * qk_dim)
    hgrn_heads = d // hgrn_o_norm_g.shape[-1]
    bf = lambda w: w.astype(BF16)

    mod = _ada_mod(c, ada_w, ada_b)
    slopes2 = jnp.asarray(
        [2.0 ** (-8.0 * (h + 1) / attn_heads) * LOG2E for h in range(attn_heads)], F32)
    q_scale = qk_dim ** -0.5 * LOG2E

    for layer in range(depth):
        m = mod[layer]
        g = norm_g[layer]
        x = _ffn(x, m, g, bf(ffn_w_gate[layer, 0]), bf(ffn_w_up[layer, 0]),
                 bf(ffn_w_down[layer, 0]), sub=0)
        kind, slot = layer % N_MIXERS, layer // N_MIXERS
        if kind == 0:
            lambda_init = 0.8 - 0.6 * math.exp(-0.3 * layer)
            gains = jnp.stack([attn_q_gain[slot], attn_k_gain[slot]])
            qkv = _proj_in(x, m, g, bf(attn_w_in[slot]), BF16, gains=gains, q_scale=q_scale)
            logit_bound = (1.02 * qk_dim * q_scale * jnp.max(jnp.abs(attn_q_gain[slot]))
                           * jnp.max(jnp.abs(attn_k_gain[slot])))
            y = _attention(qkv, slopes2, logit_bound, attn_lambda[slot], attn_subln_g[slot],
                           lambda_init, attn_heads)
            x = _proj_out(y, bf(attn_w_out[slot]), x, m)
        elif kind == 1:
            y = _conv_in(x, m, g, bf(conv_w_in[slot]), conv_w[slot])
            x = _proj_out(y, bf(conv_w_out[slot]), x, m)
        else:
            proj = _proj_in(x, m, g, bf(hgrn_w_in[slot]), F32)
            y = _hgrn_core(proj, hgrn_lb_logits, hgrn_o_norm_g[slot], layer, hgrn_heads)
            x = _proj_out(y, bf(hgrn_w_out[slot]), x, m)
        x = _ffn(x, m, g, bf(ffn_w_gate[layer, 1]), bf(ffn_w_up[layer, 1]),
                 bf(ffn_w_down[layer, 1]), sub=2)
    return x
```

```python
import functools
import math

import jax
import jax.numpy as jnp
from jax import lax
from jax.experimental import pallas as pl
from jax.experimental.pallas import tpu as pltpu

F32 = jnp.float32
BF16 = jnp.bfloat16

RMS_EPS = 1e-6
CHUNK = 64
N_MIXERS = 3
N_MOD = 9
CONV_WIDTH = 3
LOG2E = 1.4426950408889634
NEG_BIG = -1e30
F32_MIN_EXP = 127
F32_MANT_BITS = 24

LANES = 128
SUBLANES = 8
BF16_SUBLANES = 16
NORM_ROW_BLOCK = 16
V7X_VMEM_LIMIT_BYTES = 56 * 1024 * 1024

NT_DIMS = (((1,), (1,)), ((), ()))
TN_DIMS = (((0,), (0,)), ((), ()))


def _tile(n, preferred):
    for t in (preferred, 2048, 1024, 512, 256, 128, 64, 32, 16, 8):
        if t <= preferred and n % t == 0:
            return t
    return n


def _params(*sem):
    return pltpu.CompilerParams(dimension_semantics=sem,
                                vmem_limit_bytes=V7X_VMEM_LIMIT_BYTES)


def _norm_modulate_into(h_ref, x_ref, g, shift, scale):
    gain = g * (1.0 + scale)
    rows = h_ref.shape[0]
    blk = min(rows, NORM_ROW_BLOCK)
    for r in range(0, rows, blk):
        x = x_ref[0, r:r + blk, :]
        ms = jnp.mean(x * x, axis=-1, keepdims=True)
        h_ref[r:r + blk, :] = ((x * lax.rsqrt(ms + RMS_EPS)) * gain + shift).astype(BF16)


def _split3(x):
    hi = x.astype(BF16)
    r1 = x - hi.astype(F32)
    mid = r1.astype(BF16)
    lo = (r1 - mid.astype(F32)).astype(BF16)
    return hi, mid, lo


def _ada_kernel(c_ref, w_ref, b_ref, o_ref):
    c = c_ref[...]
    cond = (c * jax.nn.sigmoid(c)).astype(BF16)
    o_ref[0] = jnp.dot(cond, w_ref[0].astype(BF16),
                       preferred_element_type=F32) + b_ref[0]


def _ada_mod(c, ada_w, ada_b):
    depth, d, n = ada_w.shape
    bsz = c.shape[0]
    rows = -(-bsz // SUBLANES) * SUBLANES
    c_pad = jnp.zeros((rows, d), F32).at[:bsz].set(c)
    tn = _tile(n, 1024)
    out = pl.pallas_call(
        _ada_kernel,
        grid=(depth, n // tn),
        in_specs=[pl.BlockSpec((rows, d), lambda l, j: (0, 0)),
                  pl.BlockSpec((1, d, tn), lambda l, j: (l, 0, j)),
                  pl.BlockSpec((1, 1, tn), lambda l, j: (l, 0, j))],
        out_specs=pl.BlockSpec((1, rows, tn), lambda l, j: (l, 0, j)),
        out_shape=jax.ShapeDtypeStruct((depth, rows, n), F32),
        compiler_params=_params("parallel", "parallel"),
        name="ada_mod",
    )(c_pad, ada_w, ada_b.reshape(depth, 1, n))
    return out[:, :bsz].reshape(depth, bsz, N_MOD, d)


FFN_ROW_TILE = 1024
FFN_COL_TILE = 256


def _ffn_kernel(x_ref, mod_ref, g_ref, wg_ref, wu_ref, wd_ref, o_ref, h_ref, *, sub, row_groups):
    f = pl.program_id(2)
    r = 3 * sub

    @pl.when(f == 0)
    def _():
        _norm_modulate_into(h_ref, x_ref, g_ref[0, sub:sub + 1, :],
                            mod_ref[0, 0, r:r + 1, :], mod_ref[0, 0, r + 1:r + 2, :])
        o_ref[0] = jnp.zeros(o_ref.shape[1:], F32)

    rows = h_ref.shape[0] // row_groups
    slices = [slice(grp * rows, (grp + 1) * rows) for grp in range(row_groups)]
    gated = []
    for sl in slices:
        h = h_ref[sl, :]
        a = jnp.dot(h, wg_ref[0, 0], preferred_element_type=F32)
        b = jnp.dot(h, wu_ref[0, 0], preferred_element_type=F32)
        gated.append((a * jax.nn.sigmoid(a) * b).astype(BF16))
    for sl, p in zip(slices, gated):
        o_ref[0, sl, :] += jnp.dot(p, wd_ref[0, 0], preferred_element_type=F32)

    @pl.when(f == pl.num_programs(2) - 1)
    def _():
        o_ref[0] = x_ref[0] + (0.5 * mod_ref[0, 0, r + 2:r + 3, :]) * o_ref[0]


def _ffn(x, mod, norm_g, w_gate, w_up, w_down, layer, which):
    bsz, s, d = x.shape
    f = w_gate.shape[-1]
    tm = _tile(s, FFN_ROW_TILE)
    tf = _tile(f, FFN_COL_TILE)
    return pl.pallas_call(
        functools.partial(_ffn_kernel, sub=2 * which, row_groups=2 if tm % 256 == 0 else 1),
        grid=(bsz, s // tm, f // tf),
        in_specs=[pl.BlockSpec((1, tm, d), lambda b, i, j: (b, i, 0)),
                  pl.BlockSpec((1, 1, N_MOD, d), lambda b, i, j: (layer, b, 0, 0)),
                  pl.BlockSpec((1, 3, d), lambda b, i, j: (layer, 0, 0)),
                  pl.BlockSpec((1, 1, d, tf), lambda b, i, j: (layer, which, 0, j)),
                  pl.BlockSpec((1, 1, d, tf), lambda b, i, j: (layer, which, 0, j)),
                  pl.BlockSpec((1, 1, tf, d), lambda b, i, j: (layer, which, j, 0))],
        out_specs=pl.BlockSpec((1, tm, d), lambda b, i, j: (b, i, 0)),
        out_shape=jax.ShapeDtypeStruct((bsz, s, d), F32),
        scratch_shapes=[pltpu.VMEM((tm, d), BF16)],
        compiler_params=_params("parallel", "parallel", "arbitrary"),
        name="ffn",
    )(x, mod, norm_g, w_gate, w_up, w_down)


def _qk_norm_store(acc, gain, group_mean, o_ref, post_scale):
    width = group_mean.shape[0]
    if post_scale != 1.0:
        gain = gain * post_scale
    for j in range(acc.shape[1] // width):
        sl = slice(j * width, (j + 1) * width)
        y = acc[:, sl]
        y2 = y * y
        y2_hi = y2.astype(BF16)
        y2_lo = (y2 - y2_hi.astype(F32)).astype(BF16)
        ms = (jnp.dot(y2_hi, group_mean, preferred_element_type=F32)
              + jnp.dot(y2_lo, group_mean, preferred_element_type=F32))
        yn = (y * lax.rsqrt(ms + RMS_EPS)) * gain[:, sl]
        o_ref[0, :, sl] = yn.astype(o_ref.dtype)


def _proj_in_kernel(x_ref, mod_ref, g_ref, w_ref, *rest, qk_norm, q_scale, tiles_per_section):
    if qk_norm:
        gains_ref, gmean_ref, o_ref, h_ref = rest
    else:
        o_ref, h_ref = rest
    n = pl.program_id(2)

    @pl.when(n == 0)
    def _():
        _norm_modulate_into(h_ref, x_ref, g_ref[0, 1:2, :],
                            mod_ref[0, 0, 3:4, :], mod_ref[0, 0, 4:5, :])

    acc = jnp.dot(h_ref[...], w_ref[0], preferred_element_type=F32)
    if not qk_norm:
        o_ref[0] = acc.astype(o_ref.dtype)
        return

    section = n // tiles_per_section

    @pl.when(section == 0)
    def _():
        _qk_norm_store(acc, gains_ref[0:1, :], gmean_ref[...], o_ref, q_scale)

    @pl.when(section == 1)
    def _():
        _qk_norm_store(acc, gains_ref[1:2, :], gmean_ref[...], o_ref, 1.0)

    @pl.when(section == 2)
    def _():
        o_ref[0] = acc.astype(o_ref.dtype)


def _proj_in(x, mod, norm_g, w, layer, slot, out_dtype, gains=None, q_scale=1.0):
    bsz, s, d = x.shape
    n = w.shape[-1]
    tm = _tile(s, 1024)
    tn = _tile(d, 512)
    qk_norm = gains is not None
    in_specs = [pl.BlockSpec((1, tm, d), lambda b, i, j: (b, i, 0)),
                pl.BlockSpec((1, 1, N_MOD, d), lambda b, i, j: (layer, b, 0, 0)),
                pl.BlockSpec((1, 3, d), lambda b, i, j: (layer, 0, 0)),
                pl.BlockSpec((1, d, tn), lambda b, i, j: (slot, 0, j))]
    args = [x, mod, norm_g, w]
    if qk_norm:
        group = gains.shape[1]
        width = _tile(tn, 2 * LANES)
        blk = jnp.arange(width) // group
        gmean = jnp.where(blk[:, None] == blk[None, :], 1.0 / group, 0.0).astype(BF16)
        in_specs += [pl.BlockSpec((2, tn), lambda b, i, j: (0, 0)),
                     pl.BlockSpec((width, width), lambda b, i, j: (0, 0))]
        args += [jnp.tile(gains, (1, tn // group)), gmean]
    return pl.pallas_call(
        functools.partial(_proj_in_kernel, qk_norm=qk_norm, q_scale=q_scale,
                          tiles_per_section=d // tn),
        grid=(bsz, s // tm, n // tn),
        in_specs=in_specs,
        out_specs=pl.BlockSpec((1, tm, tn), lambda b, i, j: (b, i, j)),
        out_shape=jax.ShapeDtypeStruct((bsz, s, n), out_dtype),
        scratch_shapes=[pltpu.VMEM((tm, d), BF16)],
        compiler_params=_params("parallel", "parallel", "arbitrary"),
        name="proj_in",
    )(*args)


def _proj_out_kernel(y_ref, w_ref, x_ref, mod_ref, o_ref):
    o_ref[0] = x_ref[0] + mod_ref[0, 0, 5:6, :] * jnp.dot(
        y_ref[0], w_ref[0], preferred_element_type=F32)


def _proj_out(y, w, x, mod, layer, slot):
    bsz, s, d = x.shape
    k = y.shape[2]
    tm = _tile(s, 512)
    return pl.pallas_call(
        _proj_out_kernel,
        grid=(bsz, s // tm),
        in_specs=[pl.BlockSpec((1, tm, k), lambda b, i: (b, i, 0)),
                  pl.BlockSpec((1, k, d), lambda b, i: (slot, 0, 0)),
                  pl.BlockSpec((1, tm, d), lambda b, i: (b, i, 0)),
                  pl.BlockSpec((1, 1, N_MOD, d), lambda b, i: (layer, b, 0, 0))],
        out_specs=pl.BlockSpec((1, tm, d), lambda b, i: (b, i, 0)),
        out_shape=jax.ShapeDtypeStruct((bsz, s, d), F32),
        compiler_params=_params("parallel", "parallel"),
        name="proj_out",
    )(y, w, x, mod)


ACC_ROWS = LANES + BF16_SUBLANES
N_BIAS_COLS = 6
ATTN_TILE = 512
CARRY_SLOT = 2


def _attn_kernel(skip_ref, q_ref, qn_ref, k_ref, v_ref, slope_ref, qbias_ref, lam_ref, subg_ref,
                 dist_ref, mask_ref, o_ref, kaug_ref, vt_ref, own_ref, s_ref, mt_ref, m_ref,
                 acc_ref, *, tq, tk, lambda_init):
    h = pl.program_id(1)
    qi = pl.program_id(2)
    n_kv = vt_ref.shape[0]

    @pl.when(qi == 0)
    def _():
        own_ref[...] = slope_ref[0] * dist_ref[...] + mask_ref[...]
        row16 = lax.broadcasted_iota(jnp.int32, (BF16_SUBLANES, tk), 0)
        ones_rows = jnp.where(row16 == 0, 1.0, 0.0).astype(BF16)
        lane = lax.broadcasted_iota(jnp.int32, (tk, LANES), 1)
        row = lax.broadcasted_iota(jnp.int32, (tk, LANES), 0)
        pos = jnp.where(lane < N_BIAS_COLS // 2, row >> 1,
                        jnp.where(lane < N_BIAS_COLS, row & 1, 0)).astype(F32).astype(BF16)
        for c in range(n_kv):
            blk = v_ref[0, c * tk:(c + 1) * tk, :].astype(F32)
            vt_ref[c, 0:LANES, :] = blk.T.astype(BF16)
            vt_ref[c, LANES:ACC_ROWS, :] = ones_rows
            kaug_ref[c, :, 0:LANES] = k_ref[0, c * tk:(c + 1) * tk, :]
            kaug_ref[c, :, LANES:2 * LANES] = pos

    lo = lax.broadcasted_iota(jnp.int32, (1, LANES), 1) < LANES // 2
    q_bias = jnp.broadcast_to(qbias_ref[0], (tq, LANES)).astype(BF16)

    def augment(q):
        zero = jnp.zeros_like(q)
        return (jnp.concatenate([jnp.where(lo, q, zero), q_bias], axis=1),
                jnp.concatenate([jnp.where(lo, zero, q), q_bias], axis=1))

    def first_tile(qt):
        return jnp.maximum(qt * tq - skip_ref[h] + 1, 0) // tk

    q_aug = augment(q_ref[0])
    slope2 = slope_ref[0]
    acc_ref[...] = jnp.zeros_like(acc_ref)
    m_ref[...] = jnp.full(m_ref.shape, NEG_BIG, F32)

    def stage_a(j, slot, q_pair=q_aug):
        kt = kaug_ref[j]
        for mp in range(2):
            s = lax.dot_general(kt, q_pair[mp], NT_DIMS, preferred_element_type=F32)
            s_ref[slot, mp] = s
            mt_ref[slot, mp] = jnp.max(s, axis=0, keepdims=True)

    @pl.when(qi == 0)
    def _():
        stage_a(0, CARRY_SLOT)

    def accumulate(mp, z, tile_max, shift, vt):
        m_old = m_ref[mp]
        m_new = jnp.maximum(m_old, tile_max + shift)
        alpha = jnp.exp2(m_old - m_new)
        p = jnp.exp2(z - (m_new - shift)).astype(BF16)
        acc_ref[mp] = alpha * acc_ref[mp] + jnp.dot(vt, p, preferred_element_type=F32)
        m_ref[mp] = m_new

    def stage_b(j, slot):
        shift = slope2 * (j * tk - qi * tq).astype(F32)
        vt = vt_ref[j]
        for mp in range(2):
            accumulate(mp, s_ref[slot, mp], mt_ref[slot, mp], shift, vt)

    def stage_b_own(slot):
        vt = vt_ref[qi]
        for mp in range(2):
            z = s_ref[slot, mp] + own_ref[...]
            accumulate(mp, z, jnp.max(z, axis=0, keepdims=True), 0.0, vt)

    def finish():
        qn = jnp.minimum(qi + 1, pl.num_programs(2) - 1)
        stage_a(first_tile(qn), CARRY_SLOT, augment(qn_ref[0]))
        lam_v = lam_ref[...]
        lam = (jnp.exp(jnp.sum(lam_v[0:1] * lam_v[1:2], axis=-1, keepdims=True))
               - jnp.exp(jnp.sum(lam_v[2:3] * lam_v[3:4], axis=-1, keepdims=True))
               + lambda_init)
        o = (acc_ref[0, 0:LANES, :] / acc_ref[0, LANES:LANES + 1, :]
             - lam * (acc_ref[1, 0:LANES, :] / acc_ref[1, LANES:LANES + 1, :]))
        ms = jnp.mean(o * o, axis=0, keepdims=True)
        on = (o * lax.rsqrt(ms + RMS_EPS)).T
        o_ref[0] = ((on * subg_ref[...]) * (1.0 - lambda_init)).astype(o_ref.dtype)

    j0 = first_tile(qi)
    count = qi - j0
    rest = jnp.maximum(count - 1, 0)

    @pl.when(count == 0)
    def _():
        stage_b_own(CARRY_SLOT)
        finish()

    @pl.when(count > 0)
    def _():
        stage_a(j0 + 1, 0)
        stage_b(j0, CARRY_SLOT)

    def pair(t, carry):
        j = j0 + 1 + 2 * t
        stage_a(j + 1, 1)
        stage_b(j, 0)
        stage_a(j + 2, 0)
        stage_b(j + 1, 1)
        return carry

    lax.fori_loop(0, rest // 2, pair, 0)

    @pl.when((count > 0) & (rest % 2 == 0))
    def _():
        stage_b_own(0)
        finish()

    @pl.when((count > 0) & (rest % 2 == 1))
    def _():
        stage_a(qi, 1)
        stage_b(qi - 1, 0)
        stage_b_own(1)
        finish()


def _attention(qkv, slopes2, logit_bound, lam_vecs, subln_g, lambda_init, n_heads):
    bsz, s, d3 = qkv.shape
    d = d3 // 3
    assert d // n_heads == LANES
    tq = _tile(s, ATTN_TILE)
    tk = tq
    n_q = s // tq
    assert tk <= 2 * 256

    r = jnp.arange(tk)[:, None]
    c = jnp.arange(tq)[None, :]
    dist = jnp.minimum(0, 2 * (c - r)).astype(F32)
    mask = jnp.where(r // CHUNK <= c // CHUNK, 0.0, NEG_BIG).astype(F32)

    slope_rows = jnp.broadcast_to(slopes2[:, None, None], (n_heads, 1, tq)).astype(F32)
    hi, mid, lo = (p.astype(F32) for p in _split3(slopes2))
    pieces = jnp.stack([2 * hi, 2 * mid, 2 * lo, hi, mid, lo], axis=-1)
    qbias = jnp.zeros((n_heads, 1, LANES), F32).at[:, 0, :N_BIAS_COLS].set(pieces)
    skip = jnp.ceil((F32_MIN_EXP + F32_MANT_BITS + 2.0 * logit_bound) / slopes2)
    skip = jnp.clip(skip, 1, 2 ** 30).astype(jnp.int32)

    grid_spec = pltpu.PrefetchScalarGridSpec(
        num_scalar_prefetch=1,
        grid=(bsz, n_heads, s // tq),
        in_specs=[pl.BlockSpec((1, tq, LANES), lambda b, h, i, sk: (b, i, h)),
                  pl.BlockSpec((1, tq, LANES),
                               lambda b, h, i, sk: (b, jnp.minimum(i + 1, n_q - 1), h)),
                  pl.BlockSpec((1, s, LANES), lambda b, h, i, sk: (b, 0, n_heads + h)),
                  pl.BlockSpec((1, s, LANES), lambda b, h, i, sk: (b, 0, 2 * n_heads + h)),
                  pl.BlockSpec((1, 1, tq), lambda b, h, i, sk: (h, 0, 0)),
                  pl.BlockSpec((1, 1, LANES), lambda b, h, i, sk: (h, 0, 0)),
                  pl.BlockSpec(lam_vecs.shape, lambda b, h, i, sk: (0, 0)),
                  pl.BlockSpec((1, LANES), lambda b, h, i, sk: (0, 0)),
                  pl.BlockSpec((tk, tq), lambda b, h, i, sk: (0, 0)),
                  pl.BlockSpec((tk, tq), lambda b, h, i, sk: (0, 0))],
        out_specs=pl.BlockSpec((1, tq, LANES), lambda b, h, i, sk: (b, i, h)),
        scratch_shapes=[pltpu.VMEM((s // tk, tk, 2 * LANES), BF16),
                        pltpu.VMEM((s // tk, ACC_ROWS, tk), BF16),
                        pltpu.VMEM((tk, tq), F32),
                        pltpu.VMEM((3, 2, tk, tq), F32),
                        pltpu.VMEM((3, 2, 1, tq), F32),
                        pltpu.VMEM((2, 1, tq), F32),
                        pltpu.VMEM((2, ACC_ROWS, tq), F32)])
    return pl.pallas_call(
        functools.partial(_attn_kernel, tq=tq, tk=tk, lambda_init=lambda_init),
        grid_spec=grid_spec,
        out_shape=jax.ShapeDtypeStruct((bsz, s, d), BF16),
        compiler_params=_params("arbitrary", "arbitrary", "arbitrary"),
        name="diff_attn",
    )(skip, qkv, qkv, qkv, qkv, slope_rows, qbias, lam_vecs, subln_g.reshape(1, LANES),
      dist, mask)


CONV_HALO = BF16_SUBLANES


def _conv_in_kernel(x_ref, xh_ref, mod_ref, g_ref, wb_ref, wc_ref, wu_ref, cw_ref, o_ref,
                    h_ref, ext_ref):
    i = pl.program_id(1)
    n = pl.program_id(2)
    tm = x_ref.shape[1]
    halo = CONV_HALO

    @pl.when(n == 0)
    def _():
        g, shift, scale = g_ref[0, 1:2, :], mod_ref[0, 0, 3:4, :], mod_ref[0, 0, 4:5, :]
        _norm_modulate_into(h_ref.at[0:halo], xh_ref, g, shift, scale)
        _norm_modulate_into(h_ref.at[halo:halo + tm], x_ref, g, shift, scale)

    h_all = h_ref[...]
    c = jnp.dot(h_all, wc_ref[0], preferred_element_type=F32)
    u = jnp.dot(h_all, wu_ref[0], preferred_element_type=F32)
    b = jnp.dot(h_ref[halo:halo + tm, :], wb_ref[0], preferred_element_type=F32)
    v = c * u
    ext_ref[0:halo, :] = jnp.where(i == 0, 0.0, v[0:halo])
    ext_ref[halo:, :] = v[halo:]
    v1 = ext_ref[halo - 1:halo - 1 + tm, :]
    v2 = ext_ref[halo - 2:halo - 2 + tm, :]
    w = cw_ref[0]
    y = w[0:1] * v2 + w[1:2] * v1 + w[2:3] * v[halo:]
    o_ref[0] = (b * y).astype(o_ref.dtype)


def _conv_in(x, mod, norm_g, w_in, conv_w, layer, slot):
    bsz, s, d = x.shape
    assert CONV_WIDTH - 1 <= CONV_HALO
    tm = _tile(s, 1024)
    tn = _tile(d, 512)
    nc = d // tn
    hb = tm // CONV_HALO
    return pl.pallas_call(
        _conv_in_kernel,
        grid=(bsz, s // tm, nc),
        in_specs=[pl.BlockSpec((1, tm, d), lambda b, i, j: (b, i, 0)),
                  pl.BlockSpec((1, CONV_HALO, d),
                               lambda b, i, j: (b, jnp.maximum(i * hb - 1, 0), 0)),
                  pl.BlockSpec((1, 1, N_MOD, d), lambda b, i, j: (layer, b, 0, 0)),
                  pl.BlockSpec((1, 3, d), lambda b, i, j: (layer, 0, 0)),
                  pl.BlockSpec((1, d, tn), lambda b, i, j: (slot, 0, j)),
                  pl.BlockSpec((1, d, tn), lambda b, i, j: (slot, 0, nc + j)),
                  pl.BlockSpec((1, d, tn), lambda b, i, j: (slot, 0, 2 * nc + j)),
                  pl.BlockSpec((1, CONV_WIDTH, tn), lambda b, i, j: (slot, 0, j))],
        out_specs=pl.BlockSpec((1, tm, tn), lambda b, i, j: (b, i, j)),
        out_shape=jax.ShapeDtypeStruct((bsz, s, d), BF16),
        scratch_shapes=[pltpu.VMEM((CONV_HALO + tm, d), BF16),
                        pltpu.VMEM((CONV_HALO + tm, tn), F32)],
        compiler_params=_params("parallel", "parallel", "arbitrary"),
        name="conv_in",
    )(x, x, mod, norm_g, w_in, w_in, w_in, conv_w)


HGRN_CHUNKS_PER_STEP = 8


def _hgrn_kernel(q_ref, fl_ref, i_ref, g_ref, lbl_ref, ong_ref, o_ref,
                 state_ref, b_ref, *, layer, ts):
    t = pl.program_id(2)
    c64 = CHUNK
    sb = SUBLANES

    @pl.when(t == 0)
    def _():
        state_ref[...] = jnp.zeros_like(state_ref)

    lbl = lbl_ref[...]
    e = jnp.exp(lbl - jnp.max(lbl, axis=0, keepdims=True))
    sm = e / jnp.sum(e, axis=0, keepdims=True)
    lb = jnp.zeros((1, LANES), F32)
    for r in range(1, layer + 1):
        lb = lb + sm[r:r + 1]

    row = lax.broadcasted_iota(jnp.int32, (c64, c64), 0)
    col = lax.broadcasted_iota(jnp.int32, (c64, c64), 1)
    tri = jnp.where(row >= col, 1.0, 0.0).astype(BF16)
    ones = jnp.ones((LANES, LANES), BF16)
    sub_row = lax.broadcasted_iota(jnp.int32, (sb, LANES), 0)
    col8 = lax.broadcasted_iota(jnp.int32, (sb, c64), 1)
    groups = (2 * sb, 4 * sb, 8 * sb)
    level_masks = [(row // grp == col // grp) & (row % grp >= grp // 2) & (col % grp < grp // 2)
                   for grp in groups]


    def gates_and_cumsum(r0, u):
        fl = fl_ref[0, pl.ds(r0, c64), :]
        f = lb + (1.0 - lb) * jax.nn.sigmoid(fl)
        lf = jnp.log2(f)
        b = jnp.zeros((c64, LANES), F32)
        for piece in _split3(lf):
            b = b + jnp.dot(tri, piece, preferred_element_type=F32)
        b_ref[u] = b
        return dict(q=q_ref[0, pl.ds(r0, c64), :], kk=1.0 - f, b=b,
                    i16=i_ref[0, pl.ds(r0, c64), :].astype(BF16))

    def diagonal_sums(ch, u):
        w_rows = []
        for blk in range(c64 // sb):
            b_blk = ch["b"][blk * sb:(blk + 1) * sb]
            q_blk = ch["q"][blk * sb:(blk + 1) * sb]
            for s_ in range(sb):
                src = blk * sb + s_
                dec = jnp.exp2(b_blk - b_ref[u, src:src + 1, :])
                w_rows.append(jnp.where(sub_row >= s_,
                                        q_blk * dec * ch["kk"][src:src + 1, :], 0.0))
        w_all = jnp.concatenate(w_rows, axis=0).astype(BF16)
        ch["sums"] = jnp.dot(w_all, ones, preferred_element_type=F32)

    def level_scores(ch, u):
        q, kk, b = ch["q"], ch["kk"], ch["b"]
        out = []
        for grp in groups:
            parts = []
            for gidx in range(c64 // grp):
                edge = gidx * grp + grp // 2 - 1
                parts.append(jnp.broadcast_to(b_ref[u, edge:edge + 1, :], (grp, LANES)))
            ref = parts[0] if len(parts) == 1 else jnp.concatenate(parts, axis=0)
            qd = (q * jnp.exp2(jnp.minimum(b - ref, 0.0))).astype(BF16)
            kd = (kk * jnp.exp2(jnp.minimum(ref - b, 0.0))).astype(BF16)
            out.append(lax.dot_general(qd, kd, NT_DIMS, preferred_element_type=F32))
        ch["levels"] = out
        b_last = b_ref[u, c64 - 1:c64, :]
        kd_end = (kk * jnp.exp2(b_last - b)).astype(BF16)
        ch["upd"] = lax.dot_general(ch["i16"], kd_end, TN_DIMS,
                                    preferred_element_type=F32)
        ch["decay"] = jnp.exp2(b_last)
        ch["q_in"] = (q * jnp.exp2(b)).astype(BF16)

    def intra_chunk(ch):
        score_rows = []
        for blk in range(c64 // sb):
            acc = jnp.zeros((sb, c64), F32)
            for s_ in range(sb):
                idx = blk * sb + s_
                acc = jnp.where(col8 == idx, ch["sums"][idx * sb:(idx + 1) * sb, :c64], acc)
            score_rows.append(acc)
        scores = jnp.concatenate(score_rows, axis=0)
        for lvl in range(len(groups)):
            scores = scores + jnp.where(level_masks[lvl], ch["levels"][lvl], 0.0)
        ch["intra"] = jnp.dot(scores.astype(BF16), ch["i16"], preferred_element_type=F32)

    per_step = b_ref.shape[0]

    def step(c, carry):
        base = c * (per_step * c64)
        starts = [pl.multiple_of(base + u * c64, c64) for u in range(per_step)]
        chunks = [gates_and_cumsum(r0, u) for u, r0 in enumerate(starts)]
        for u, ch in enumerate(chunks):
            diagonal_sums(ch, u)
        for u, ch in enumerate(chunks):
            level_scores(ch, u)
        for ch in chunks:
            intra_chunk(ch)
        st = state_ref[...]
        for r0, ch in zip(starts, chunks):
            inter = lax.dot_general(ch["q_in"], st.astype(BF16), NT_DIMS,
                                    preferred_element_type=F32)
            st = st * ch["decay"] + ch["upd"]
            o = inter + ch["intra"]
            ms = jnp.mean(o * o, axis=-1, keepdims=True)
            on = (o * lax.rsqrt(ms + RMS_EPS)) * ong_ref[...]
            gv = g_ref[0, pl.ds(r0, c64), :]
            o_ref[0, pl.ds(r0, c64), :] = (on * (gv * jax.nn.sigmoid(gv))).astype(o_ref.dtype)
        state_ref[...] = st
        return carry

    lax.fori_loop(0, ts // (per_step * c64), step, 0)


def _hgrn_core(proj, lb_logits, o_norm_g, layer, n_heads):
    bsz, s, d4 = proj.shape
    d = d4 // 4
    assert d // n_heads == LANES
    ts = _tile(s, 512)
    per_step = math.gcd(HGRN_CHUNKS_PER_STEP, ts // CHUNK)
    depth = lb_logits.shape[0]
    return pl.pallas_call(
        functools.partial(_hgrn_kernel, layer=layer, ts=ts),
        grid=(bsz, n_heads, s // ts),
        in_specs=[pl.BlockSpec((1, ts, LANES), lambda b, h, t: (b, t, h)),
                  pl.BlockSpec((1, ts, LANES), lambda b, h, t: (b, t, n_heads + h)),
                  pl.BlockSpec((1, ts, LANES), lambda b, h, t: (b, t, 2 * n_heads + h)),
                  pl.BlockSpec((1, ts, LANES), lambda b, h, t: (b, t, 3 * n_heads + h)),
                  pl.BlockSpec((depth, LANES), lambda b, h, t: (0, h)),
                  pl.BlockSpec((1, LANES), lambda b, h, t: (0, 0))],
        out_specs=pl.BlockSpec((1, ts, LANES), lambda b, h, t: (b, t, h)),
        out_shape=jax.ShapeDtypeStruct((bsz, s, d), BF16),
        scratch_shapes=[pltpu.VMEM((LANES, LANES), F32),
                        pltpu.VMEM((per_step, CHUNK, LANES), F32)],
        compiler_params=_params("parallel", "parallel", "arbitrary"),
        name="hgrn2",
    )(proj, proj, proj, proj, lb_logits, o_norm_g.reshape(1, LANES))


def kernel(x, c, ada_w, ada_b, norm_g, ffn_w_gate, ffn_w_up, ffn_w_down,
           attn_w_in, attn_w_out, attn_q_gain, attn_k_gain, attn_lambda, attn_subln_g,
           conv_w_in, conv_w, conv_w_out,
           hgrn_w_in, hgrn_w_out, hgrn_o_norm_g, hgrn_lb_logits):
    depth = ada_w.shape[0]
    d = x.shape[-1]
    qk_dim = attn_q_gain.shape[-1]
    attn_heads = d // (2 * qk_dim)
    hgrn_heads = d // hgrn_o_norm_g.shape[-1]
    ffn_w_gate, ffn_w_up, ffn_w_down = (w.astype(BF16) for w in (ffn_w_gate, ffn_w_up, ffn_w_down))
    attn_w_in, attn_w_out = attn_w_in.astype(BF16), attn_w_out.astype(BF16)
    conv_w_in, conv_w_out = conv_w_in.astype(BF16), conv_w_out.astype(BF16)
    hgrn_w_in, hgrn_w_out = hgrn_w_in.astype(BF16), hgrn_w_out.astype(BF16)

    mod = _ada_mod(c, ada_w, ada_b)
    slopes2 = jnp.asarray(
        [2.0 ** (-8.0 * (h + 1) / attn_heads) * LOG2E for h in range(attn_heads)], F32)
    q_scale = qk_dim ** -0.5 * LOG2E

    for layer in range(depth):
        x = _ffn(x, mod, norm_g, ffn_w_gate, ffn_w_up, ffn_w_down, layer, 0)
        kind, slot = layer % N_MIXERS, layer // N_MIXERS
        if kind == 0:
            lambda_init = 0.8 - 0.6 * math.exp(-0.3 * layer)
            gains = jnp.stack([attn_q_gain[slot], attn_k_gain[slot]])
            qkv = _proj_in(x, mod, norm_g, attn_w_in, layer, slot, BF16,
                           gains=gains, q_scale=q_scale)
            logit_bound = (1.02 * qk_dim * q_scale * jnp.max(jnp.abs(attn_q_gain[slot]))
                           * jnp.max(jnp.abs(attn_k_gain[slot])))
            y = _attention(qkv, slopes2, logit_bound, attn_lambda[slot], attn_subln_g[slot],
                           lambda_init, attn_heads)
            x = _proj_out(y, attn_w_out, x, mod, layer, slot)
        elif kind == 1:
            y = _conv_in(x, mod, norm_g, conv_w_in, conv_w, layer, slot)
            x = _proj_out(y, conv_w_out, x, mod, layer, slot)
        else:
            proj = _proj_in(x, mod, norm_g, hgrn_w_in, layer, slot, F32)
            y = _hgrn_core(proj, hgrn_lb_logits, hgrn_o_norm_g[slot], layer, hgrn_heads)
            x = _proj_out(y, hgrn_w_out, x, mod, layer, slot)
        x = _ffn(x, mod, norm_g, ffn_w_gate, ffn_w_up, ffn_w_down, layer, 1)
    return x
```

```python
import functools
import math

import jax
import jax.numpy as jnp
from jax import lax
from jax.experimental import pallas as pl
from jax.experimental.pallas import tpu as pltpu

F32 = jnp.float32
BF16 = jnp.bfloat16

RMS_EPS = 1e-6
CHUNK = 64
N_MIXERS = 3
N_MOD = 9
CONV_WIDTH = 3
LOG2E = 1.4426950408889634
NEG_BIG = -1e30
F32_MIN_EXP = 127
F32_MANT_BITS = 24

LANES = 128
SUBLANES = 8
BF16_SUBLANES = 16
NORM_ROW_BLOCK = 16
V7X_VMEM_LIMIT_BYTES = 60 * 1024 * 1024

NT_DIMS = (((1,), (1,)), ((), ()))
TN_DIMS = (((0,), (0,)), ((), ()))


def _tile(n, preferred):
    for t in (preferred, 2048, 1024, 512, 256, 128, 64, 32, 16, 8):
        if t <= preferred and n % t == 0:
            return t
    return n


def _params(*sem):
    return pltpu.CompilerParams(dimension_semantics=sem,
                                vmem_limit_bytes=V7X_VMEM_LIMIT_BYTES)


def _norm_modulate_into(h_ref, x_ref, g, shift, scale):
    gain = g * (1.0 + scale)
    rows = h_ref.shape[0]
    blk = min(rows, NORM_ROW_BLOCK)
    for r in range(0, rows, blk):
        x = x_ref[0, r:r + blk, :]
        ms = jnp.mean(x * x, axis=-1, keepdims=True)
        h_ref[r:r + blk, :] = ((x * lax.rsqrt(ms + RMS_EPS)) * gain + shift).astype(BF16)


def _split3(x):
    hi = x.astype(BF16)
    r1 = x - hi.astype(F32)
    mid = r1.astype(BF16)
    lo = (r1 - mid.astype(F32)).astype(BF16)
    return hi, mid, lo


def _ada_kernel(c_ref, w_ref, b_ref, o_ref):
    c = c_ref[...]
    cond = (c * jax.nn.sigmoid(c)).astype(BF16)
    o_ref[0] = jnp.dot(cond, w_ref[0].astype(BF16),
                       preferred_element_type=F32) + b_ref[0]


def _ada_mod(c, ada_w, ada_b):
    depth, d, n = ada_w.shape
    bsz = c.shape[0]
    rows = -(-bsz // SUBLANES) * SUBLANES
    c_pad = jnp.zeros((rows, d), F32).at[:bsz].set(c)
    tn = _tile(n, 1024)
    out = pl.pallas_call(
        _ada_kernel,
        grid=(depth, n // tn),
        in_specs=[pl.BlockSpec((rows, d), lambda l, j: (0, 0)),
                  pl.BlockSpec((1, d, tn), lambda l, j: (l, 0, j)),
                  pl.BlockSpec((1, 1, tn), lambda l, j: (l, 0, j))],
        out_specs=pl.BlockSpec((1, rows, tn), lambda l, j: (l, 0, j)),
        out_shape=jax.ShapeDtypeStruct((depth, rows, n), F32),
        compiler_params=_params("parallel", "parallel"),
        name="ada_mod",
    )(c_pad, ada_w, ada_b.reshape(depth, 1, n))
    return out[:, :bsz].reshape(depth, bsz, N_MOD, d)


FFN_ROW_TILE = 1024
FFN_COL_TILE = 512


def _ffn_kernel(x_ref, mod_ref, g_ref, wg_ref, wu_ref, wd_ref, o_ref, h_ref, *, sub, row_groups):
    f = pl.program_id(2)
    r = 3 * sub

    @pl.when(f == 0)
    def _():
        _norm_modulate_into(h_ref, x_ref, g_ref[0, sub:sub + 1, :],
                            mod_ref[0, 0, r:r + 1, :], mod_ref[0, 0, r + 1:r + 2, :])
        o_ref[0] = jnp.zeros(o_ref.shape[1:], F32)

    rows = h_ref.shape[0] // row_groups
    slices = [slice(grp * rows, (grp + 1) * rows) for grp in range(row_groups)]
    gated = []
    for sl in slices:
        h = h_ref[sl, :]
        a = jnp.dot(h, wg_ref[0, 0], preferred_element_type=F32)
        b = jnp.dot(h, wu_ref[0, 0], preferred_element_type=F32)
        gated.append((a * jax.nn.sigmoid(a) * b).astype(BF16))
    for sl, p in zip(slices, gated):
        o_ref[0, sl, :] += jnp.dot(p, wd_ref[0, 0], preferred_element_type=F32)

    @pl.when(f == pl.num_programs(2) - 1)
    def _():
        o_ref[0] = x_ref[0] + (0.5 * mod_ref[0, 0, r + 2:r + 3, :]) * o_ref[0]


def _ffn(x, mod, norm_g, w_gate, w_up, w_down, layer, which):
    bsz, s, d = x.shape
    f = w_gate.shape[-1]
    tm = _tile(s, FFN_ROW_TILE)
    tf = _tile(f, FFN_COL_TILE)
    return pl.pallas_call(
        functools.partial(_ffn_kernel, sub=2 * which, row_groups=2 if tm % 256 == 0 else 1),
        grid=(bsz, s // tm, f // tf),
        in_specs=[pl.BlockSpec((1, tm, d), lambda b, i, j: (b, i, 0)),
                  pl.BlockSpec((1, 1, N_MOD, d), lambda b, i, j: (layer, b, 0, 0)),
                  pl.BlockSpec((1, 3, d), lambda b, i, j: (layer, 0, 0)),
                  pl.BlockSpec((1, 1, d, tf), lambda b, i, j: (layer, which, 0, j)),
                  pl.BlockSpec((1, 1, d, tf), lambda b, i, j: (layer, which, 0, j)),
                  pl.BlockSpec((1, 1, tf, d), lambda b, i, j: (layer, which, j, 0))],
        out_specs=pl.BlockSpec((1, tm, d), lambda b, i, j: (b, i, 0)),
        out_shape=jax.ShapeDtypeStruct((bsz, s, d), F32),
        scratch_shapes=[pltpu.VMEM((tm, d), BF16)],
        compiler_params=_params("parallel", "parallel", "arbitrary"),
        name="ffn",
    )(x, mod, norm_g, w_gate, w_up, w_down)


def _qk_norm_store(acc, gain, group_mean, o_ref, post_scale):
    width = group_mean.shape[0]
    if post_scale != 1.0:
        gain = gain * post_scale
    for j in range(acc.shape[1] // width):
        sl = slice(j * width, (j + 1) * width)
        y = acc[:, sl]
        y2 = y * y
        y2_hi = y2.astype(BF16)
        y2_lo = (y2 - y2_hi.astype(F32)).astype(BF16)
        ms = (jnp.dot(y2_hi, group_mean, preferred_element_type=F32)
              + jnp.dot(y2_lo, group_mean, preferred_element_type=F32))
        yn = (y * lax.rsqrt(ms + RMS_EPS)) * gain[:, sl]
        o_ref[0, :, sl] = yn.astype(o_ref.dtype)


def _proj_in_kernel(x_ref, mod_ref, g_ref, w_ref, *rest, qk_norm, q_scale, tiles_per_section):
    if qk_norm:
        gains_ref, gmean_ref, o_ref, h_ref = rest
    else:
        o_ref, h_ref = rest
    n = pl.program_id(2)

    @pl.when(n == 0)
    def _():
        _norm_modulate_into(h_ref, x_ref, g_ref[0, 1:2, :],
                            mod_ref[0, 0, 3:4, :], mod_ref[0, 0, 4:5, :])

    acc = jnp.dot(h_ref[...], w_ref[0], preferred_element_type=F32)
    if not qk_norm:
        o_ref[0] = acc.astype(o_ref.dtype)
        return

    section = n // tiles_per_section

    @pl.when(section == 0)
    def _():
        _qk_norm_store(acc, gains_ref[0:1, :], gmean_ref[...], o_ref, q_scale)

    @pl.when(section == 1)
    def _():
        _qk_norm_store(acc, gains_ref[1:2, :], gmean_ref[...], o_ref, 1.0)

    @pl.when(section == 2)
    def _():
        o_ref[0] = acc.astype(o_ref.dtype)


def _proj_in(x, mod, norm_g, w, layer, slot, out_dtype, gains=None, q_scale=1.0):
    bsz, s, d = x.shape
    n = w.shape[-1]
    tm = _tile(s, 1024)
    tn = _tile(d, 1024)
    qk_norm = gains is not None
    in_specs = [pl.BlockSpec((1, tm, d), lambda b, i, j: (b, i, 0)),
                pl.BlockSpec((1, 1, N_MOD, d), lambda b, i, j: (layer, b, 0, 0)),
                pl.BlockSpec((1, 3, d), lambda b, i, j: (layer, 0, 0)),
                pl.BlockSpec((1, d, tn), lambda b, i, j: (slot, 0, j))]
    args = [x, mod, norm_g, w]
    if qk_norm:
        group = gains.shape[1]
        width = _tile(tn, 2 * LANES)
        blk = jnp.arange(width) // group
        gmean = jnp.where(blk[:, None] == blk[None, :], 1.0 / group, 0.0).astype(BF16)
        in_specs += [pl.BlockSpec((2, tn), lambda b, i, j: (0, 0)),
                     pl.BlockSpec((width, width), lambda b, i, j: (0, 0))]
        args += [jnp.tile(gains, (1, tn // group)), gmean]
    return pl.pallas_call(
        functools.partial(_proj_in_kernel, qk_norm=qk_norm, q_scale=q_scale,
                          tiles_per_section=d // tn),
        grid=(bsz, s // tm, n // tn),
        in_specs=in_specs,
        out_specs=pl.BlockSpec((1, tm, tn), lambda b, i, j: (b, i, j)),
        out_shape=jax.ShapeDtypeStruct((bsz, s, n), out_dtype),
        scratch_shapes=[pltpu.VMEM((tm, d), BF16)],
        compiler_params=_params("parallel", "parallel", "arbitrary"),
        name="proj_in",
    )(*args)


def _proj_out_kernel(y_ref, w_ref, x_ref, mod_ref, o_ref):
    o_ref[0] = x_ref[0] + mod_ref[0, 0, 5:6, :] * jnp.dot(
        y_ref[0], w_ref[0], preferred_element_type=F32)


def _proj_out(y, w, x, mod, layer, slot):
    bsz, s, d = x.shape
    k = y.shape[2]
    tm = _tile(s, 512)
    return pl.pallas_call(
        _proj_out_kernel,
        grid=(bsz, s // tm),
        in_specs=[pl.BlockSpec((1, tm, k), lambda b, i: (b, i, 0)),
                  pl.BlockSpec((1, k, d), lambda b, i: (slot, 0, 0)),
                  pl.BlockSpec((1, tm, d), lambda b, i: (b, i, 0)),
                  pl.BlockSpec((1, 1, N_MOD, d), lambda b, i: (layer, b, 0, 0))],
        out_specs=pl.BlockSpec((1, tm, d), lambda b, i: (b, i, 0)),
        out_shape=jax.ShapeDtypeStruct((bsz, s, d), F32),
        compiler_params=_params("parallel", "parallel"),
        name="proj_out",
    )(y, w, x, mod)


ACC_ROWS = LANES + BF16_SUBLANES
N_BIAS_COLS = 6
ATTN_TILE = 512
CARRY_SLOT = 2


def _attn_kernel(skip_ref, q_ref, qn_ref, k_ref, v_ref, slope_ref, qbias_ref, lam_ref, subg_ref,
                 dist_ref, mask_ref, o_ref, kaug_ref, vt_ref, own_ref, s_ref, mt_ref, m_ref,
                 acc_ref, *, tq, tk, lambda_init):
    h = pl.program_id(1)
    qi = pl.program_id(2)
    n_kv = vt_ref.shape[0]

    @pl.when(qi == 0)
    def _():
        own_ref[...] = slope_ref[0] * dist_ref[...] + mask_ref[...]
        row16 = lax.broadcasted_iota(jnp.int32, (BF16_SUBLANES, tk), 0)
        ones_rows = jnp.where(row16 == 0, 1.0, 0.0).astype(BF16)
        lane = lax.broadcasted_iota(jnp.int32, (tk, LANES), 1)
        row = lax.broadcasted_iota(jnp.int32, (tk, LANES), 0)
        pos = jnp.where(lane < N_BIAS_COLS // 2, row >> 1,
                        jnp.where(lane < N_BIAS_COLS, row & 1, 0)).astype(F32).astype(BF16)
        for c in range(n_kv):
            blk = v_ref[0, c * tk:(c + 1) * tk, :].astype(F32)
            vt_ref[c, 0:LANES, :] = blk.T.astype(BF16)
            vt_ref[c, LANES:ACC_ROWS, :] = ones_rows
            kaug_ref[c, :, 0:LANES] = k_ref[0, c * tk:(c + 1) * tk, :]
            kaug_ref[c, :, LANES:2 * LANES] = pos

    lo = lax.broadcasted_iota(jnp.int32, (1, LANES), 1) < LANES // 2
    q_bias = jnp.broadcast_to(qbias_ref[0], (tq, LANES)).astype(BF16)

    def augment(q):
        zero = jnp.zeros_like(q)
        return (jnp.concatenate([jnp.where(lo, q, zero), q_bias], axis=1),
                jnp.concatenate([jnp.where(lo, zero, q), q_bias], axis=1))

    def first_tile(qt):
        return jnp.maximum(qt * tq - skip_ref[h] + 1, 0) // tk

    q_aug = augment(q_ref[0])
    slope2 = slope_ref[0]
    acc_ref[...] = jnp.zeros_like(acc_ref)
    m_ref[...] = jnp.full(m_ref.shape, NEG_BIG, F32)

    def stage_a(j, slot, q_pair=q_aug):
        kt = kaug_ref[j]
        for mp in range(2):
            s = lax.dot_general(kt, q_pair[mp], NT_DIMS, preferred_element_type=F32)
            s_ref[slot, mp] = s
            mt_ref[slot, mp] = jnp.max(s, axis=0, keepdims=True)

    @pl.when(qi == 0)
    def _():
        stage_a(0, CARRY_SLOT)

    def accumulate(mp, z, tile_max, shift, vt):
        m_old = m_ref[mp]
        m_new = jnp.maximum(m_old, tile_max + shift)
        alpha = jnp.exp2(m_old - m_new)
        p = jnp.exp2(z - (m_new - shift)).astype(BF16)
        acc_ref[mp] = alpha * acc_ref[mp] + jnp.dot(vt, p, preferred_element_type=F32)
        m_ref[mp] = m_new

    def stage_b(j, slot):
        shift = slope2 * (j * tk - qi * tq).astype(F32)
        vt = vt_ref[j]
        for mp in range(2):
            accumulate(mp, s_ref[slot, mp], mt_ref[slot, mp], shift, vt)

    def stage_b_own(slot):
        vt = vt_ref[qi]
        for mp in range(2):
            z = s_ref[slot, mp] + own_ref[...]
            accumulate(mp, z, jnp.max(z, axis=0, keepdims=True), 0.0, vt)

    def finish():
        qn = jnp.minimum(qi + 1, pl.num_programs(2) - 1)
        stage_a(first_tile(qn), CARRY_SLOT, augment(qn_ref[0]))
        lam_v = lam_ref[...]
        lam = (jnp.exp(jnp.sum(lam_v[0:1] * lam_v[1:2], axis=-1, keepdims=True))
               - jnp.exp(jnp.sum(lam_v[2:3] * lam_v[3:4], axis=-1, keepdims=True))
               + lambda_init)
        o = (acc_ref[0, 0:LANES, :] / acc_ref[0, LANES:LANES + 1, :]
             - lam * (acc_ref[1, 0:LANES, :] / acc_ref[1, LANES:LANES + 1, :]))
        ms = jnp.mean(o * o, axis=0, keepdims=True)
        on = (o * lax.rsqrt(ms + RMS_EPS)).T
        o_ref[0] = ((on * subg_ref[...]) * (1.0 - lambda_init)).astype(o_ref.dtype)

    j0 = first_tile(qi)
    count = qi - j0
    rest = jnp.maximum(count - 1, 0)

    @pl.when(count == 0)
    def _():
        stage_b_own(CARRY_SLOT)
        finish()

    @pl.when(count > 0)
    def _():
        stage_a(j0 + 1, 0)
        stage_b(j0, CARRY_SLOT)

    def pair_at(j):
        stage_a(j + 1, 1)
        stage_b(j, 0)
        stage_a(j + 2, 0)
        stage_b(j + 1, 1)

    def quad(t, carry):
        j = j0 + 1 + 4 * t
        pair_at(j)
        pair_at(j + 2)
        return carry

    def pair(t, carry):
        pair_at(j0 + 1 + 4 * (rest // 4) + 2 * t)
        return carry

    lax.fori_loop(0, rest // 4, quad, 0)
    lax.fori_loop(0, (rest % 4) // 2, pair, 0)

    @pl.when((count > 0) & (rest % 2 == 0))
    def _():
        stage_b_own(0)
        finish()

    @pl.when((count > 0) & (rest % 2 == 1))
    def _():
        stage_a(qi, 1)
        stage_b(qi - 1, 0)
        stage_b_own(1)
        finish()


def _attention(qkv, slopes2, logit_bound, lam_vecs, subln_g, lambda_init, n_heads):
    bsz, s, d3 = qkv.shape
    d = d3 // 3
    assert d // n_heads == LANES
    tq = _tile(s, ATTN_TILE)
    tk = tq
    n_q = s // tq
    assert tk <= 2 * 256

    r = jnp.arange(tk)[:, None]
    c = jnp.arange(tq)[None, :]
    dist = jnp.minimum(0, 2 * (c - r)).astype(F32)
    mask = jnp.where(r // CHUNK <= c // CHUNK, 0.0, NEG_BIG).astype(F32)

    slope_rows = jnp.broadcast_to(slopes2[:, None, None], (n_heads, 1, tq)).astype(F32)
    hi, mid, lo = (p.astype(F32) for p in _split3(slopes2))
    pieces = jnp.stack([2 * hi, 2 * mid, 2 * lo, hi, mid, lo], axis=-1)
    qbias = jnp.zeros((n_heads, 1, LANES), F32).at[:, 0, :N_BIAS_COLS].set(pieces)
    skip = jnp.ceil((F32_MIN_EXP + F32_MANT_BITS + 2.0 * logit_bound) / slopes2)
    skip = jnp.clip(skip, 1, 2 ** 30).astype(jnp.int32)

    grid_spec = pltpu.PrefetchScalarGridSpec(
        num_scalar_prefetch=1,
        grid=(bsz, n_heads, s // tq),
        in_specs=[pl.BlockSpec((1, tq, LANES), lambda b, h, i, sk: (b, i, h)),
                  pl.BlockSpec((1, tq, LANES),
                               lambda b, h, i, sk: (b, jnp.minimum(i + 1, n_q - 1), h)),
                  pl.BlockSpec((1, s, LANES), lambda b, h, i, sk: (b, 0, n_heads + h)),
                  pl.BlockSpec((1, s, LANES), lambda b, h, i, sk: (b, 0, 2 * n_heads + h)),
                  pl.BlockSpec((1, 1, tq), lambda b, h, i, sk: (h, 0, 0)),
                  pl.BlockSpec((1, 1, LANES), lambda b, h, i, sk: (h, 0, 0)),
                  pl.BlockSpec(lam_vecs.shape, lambda b, h, i, sk: (0, 0)),
                  pl.BlockSpec((1, LANES), lambda b, h, i, sk: (0, 0)),
                  pl.BlockSpec((tk, tq), lambda b, h, i, sk: (0, 0)),
                  pl.BlockSpec((tk, tq), lambda b, h, i, sk: (0, 0))],
        out_specs=pl.BlockSpec((1, tq, LANES), lambda b, h, i, sk: (b, i, h)),
        scratch_shapes=[pltpu.VMEM((s // tk, tk, 2 * LANES), BF16),
                        pltpu.VMEM((s // tk, ACC_ROWS, tk), BF16),
                        pltpu.VMEM((tk, tq), F32),
                        pltpu.VMEM((3, 2, tk, tq), F32),
                        pltpu.VMEM((3, 2, 1, tq), F32),
                        pltpu.VMEM((2, 1, tq), F32),
                        pltpu.VMEM((2, ACC_ROWS, tq), F32)])
    return pl.pallas_call(
        functools.partial(_attn_kernel, tq=tq, tk=tk, lambda_init=lambda_init),
        grid_spec=grid_spec,
        out_shape=jax.ShapeDtypeStruct((bsz, s, d), BF16),
        compiler_params=_params("arbitrary", "arbitrary", "arbitrary"),
        name="diff_attn",
    )(skip, qkv, qkv, qkv, qkv, slope_rows, qbias, lam_vecs, subln_g.reshape(1, LANES),
      dist, mask)


CONV_HALO = BF16_SUBLANES


def _conv_in_kernel(x_ref, xh_ref, mod_ref, g_ref, wb_ref, wc_ref, wu_ref, cw_ref, o_ref,
                    h_ref, ext_ref):
    i = pl.program_id(1)
    n = pl.program_id(2)
    tm = x_ref.shape[1]
    halo = CONV_HALO

    @pl.when(n == 0)
    def _():
        g, shift, scale = g_ref[0, 1:2, :], mod_ref[0, 0, 3:4, :], mod_ref[0, 0, 4:5, :]
        _norm_modulate_into(h_ref.at[0:halo], xh_ref, g, shift, scale)
        _norm_modulate_into(h_ref.at[halo:halo + tm], x_ref, g, shift, scale)

    h_all = h_ref[...]
    c = jnp.dot(h_all, wc_ref[0], preferred_element_type=F32)
    u = jnp.dot(h_all, wu_ref[0], preferred_element_type=F32)
    b = jnp.dot(h_ref[halo:halo + tm, :], wb_ref[0], preferred_element_type=F32)
    v = c * u
    ext_ref[0:halo, :] = jnp.where(i == 0, 0.0, v[0:halo])
    ext_ref[halo:, :] = v[halo:]
    v1 = ext_ref[halo - 1:halo - 1 + tm, :]
    v2 = ext_ref[halo - 2:halo - 2 + tm, :]
    w = cw_ref[0]
    y = w[0:1] * v2 + w[1:2] * v1 + w[2:3] * v[halo:]
    o_ref[0] = (b * y).astype(o_ref.dtype)


def _conv_in(x, mod, norm_g, w_in, conv_w, layer, slot):
    bsz, s, d = x.shape
    assert CONV_WIDTH - 1 <= CONV_HALO
    tm = _tile(s, 1024)
    tn = _tile(d, 512)
    nc = d // tn
    hb = tm // CONV_HALO
    return pl.pallas_call(
        _conv_in_kernel,
        grid=(bsz, s // tm, nc),
        in_specs=[pl.BlockSpec((1, tm, d), lambda b, i, j: (b, i, 0)),
                  pl.BlockSpec((1, CONV_HALO, d),
                               lambda b, i, j: (b, jnp.maximum(i * hb - 1, 0), 0)),
                  pl.BlockSpec((1, 1, N_MOD, d), lambda b, i, j: (layer, b, 0, 0)),
                  pl.BlockSpec((1, 3, d), lambda b, i, j: (layer, 0, 0)),
                  pl.BlockSpec((1, d, tn), lambda b, i, j: (slot, 0, j)),
                  pl.BlockSpec((1, d, tn), lambda b, i, j: (slot, 0, nc + j)),
                  pl.BlockSpec((1, d, tn), lambda b, i, j: (slot, 0, 2 * nc + j)),
                  pl.BlockSpec((1, CONV_WIDTH, tn), lambda b, i, j: (slot, 0, j))],
        out_specs=pl.BlockSpec((1, tm, tn), lambda b, i, j: (b, i, j)),
        out_shape=jax.ShapeDtypeStruct((bsz, s, d), BF16),
        scratch_shapes=[pltpu.VMEM((CONV_HALO + tm, d), BF16),
                        pltpu.VMEM((CONV_HALO + tm, tn), F32)],
        compiler_params=_params("parallel", "parallel", "arbitrary"),
        name="conv_in",
    )(x, x, mod, norm_g, w_in, w_in, w_in, conv_w)


HGRN_CHUNKS_PER_STEP = 8


def _hgrn_kernel(q_ref, fl_ref, i_ref, g_ref, lbl_ref, ong_ref, o_ref,
                 state_ref, b_ref, *, layer, ts):
    t = pl.program_id(2)
    c64 = CHUNK
    sb = SUBLANES

    @pl.when(t == 0)
    def _():
        state_ref[...] = jnp.zeros_like(state_ref)

    lbl = lbl_ref[...]
    e = jnp.exp(lbl - jnp.max(lbl, axis=0, keepdims=True))
    sm = e / jnp.sum(e, axis=0, keepdims=True)
    lb = jnp.zeros((1, LANES), F32)
    for r in range(1, layer + 1):
        lb = lb + sm[r:r + 1]

    row = lax.broadcasted_iota(jnp.int32, (c64, c64), 0)
    col = lax.broadcasted_iota(jnp.int32, (c64, c64), 1)
    tri = jnp.where(row >= col, 1.0, 0.0).astype(BF16)
    ones = jnp.ones((LANES, LANES), BF16)
    sub_row = lax.broadcasted_iota(jnp.int32, (sb, LANES), 0)
    col8 = lax.broadcasted_iota(jnp.int32, (sb, c64), 1)
    groups = (2 * sb, 4 * sb, 8 * sb)
    level_masks = [(row // grp == col // grp) & (row % grp >= grp // 2) & (col % grp < grp // 2)
                   for grp in groups]


    def gates_and_cumsum(r0, u):
        fl = fl_ref[0, pl.ds(r0, c64), :]
        f = lb + (1.0 - lb) * jax.nn.sigmoid(fl)
        lf = jnp.log2(f)
        b = jnp.zeros((c64, LANES), F32)
        for piece in _split3(lf):
            b = b + jnp.dot(tri, piece, preferred_element_type=F32)
        b_ref[u] = b
        return dict(q=q_ref[0, pl.ds(r0, c64), :], kk=1.0 - f, b=b,
                    i16=i_ref[0, pl.ds(r0, c64), :].astype(BF16))

    def diagonal_sums(ch, u):
        w_rows = []
        for blk in range(c64 // sb):
            b_blk = ch["b"][blk * sb:(blk + 1) * sb]
            q_blk = ch["q"][blk * sb:(blk + 1) * sb]
            for s_ in range(sb):
                src = blk * sb + s_
                dec = jnp.exp2(b_blk - b_ref[u, src:src + 1, :])
                w_rows.append(jnp.where(sub_row >= s_,
                                        q_blk * dec * ch["kk"][src:src + 1, :], 0.0))
        w_all = jnp.concatenate(w_rows, axis=0).astype(BF16)
        ch["sums"] = jnp.dot(w_all, ones, preferred_element_type=F32)

    def level_scores(ch, u):
        q, kk, b = ch["q"], ch["kk"], ch["b"]
        out = []
        for grp in groups:
            parts = []
            for gidx in range(c64 // grp):
                edge = gidx * grp + grp // 2 - 1
                parts.append(jnp.broadcast_to(b_ref[u, edge:edge + 1, :], (grp, LANES)))
            ref = parts[0] if len(parts) == 1 else jnp.concatenate(parts, axis=0)
            qd = (q * jnp.exp2(jnp.minimum(b - ref, 0.0))).astype(BF16)
            kd = (kk * jnp.exp2(jnp.minimum(ref - b, 0.0))).astype(BF16)
            out.append(lax.dot_general(qd, kd, NT_DIMS, preferred_element_type=F32))
        ch["levels"] = out
        b_last = b_ref[u, c64 - 1:c64, :]
        kd_end = (kk * jnp.exp2(b_last - b)).astype(BF16)
        ch["upd"] = lax.dot_general(ch["i16"], kd_end, TN_DIMS,
                                    preferred_element_type=F32)
        ch["decay"] = jnp.exp2(b_last)
        ch["q_in"] = (q * jnp.exp2(b)).astype(BF16)

    def intra_chunk(ch):
        score_rows = []
        for blk in range(c64 // sb):
            acc = jnp.zeros((sb, c64), F32)
            for s_ in range(sb):
                idx = blk * sb + s_
                acc = jnp.where(col8 == idx, ch["sums"][idx * sb:(idx + 1) * sb, :c64], acc)
            score_rows.append(acc)
        scores = jnp.concatenate(score_rows, axis=0)
        for lvl in range(len(groups)):
            scores = scores + jnp.where(level_masks[lvl], ch["levels"][lvl], 0.0)
        ch["intra"] = jnp.dot(scores.astype(BF16), ch["i16"], preferred_element_type=F32)

    per_step = b_ref.shape[0]

    def step(c, carry):
        base = c * (per_step * c64)
        starts = [pl.multiple_of(base + u * c64, c64) for u in range(per_step)]
        chunks = [gates_and_cumsum(r0, u) for u, r0 in enumerate(starts)]
        for u, ch in enumerate(chunks):
            diagonal_sums(ch, u)
        for u, ch in enumerate(chunks):
            level_scores(ch, u)
        for ch in chunks:
            intra_chunk(ch)
        st = state_ref[...]
        for r0, ch in zip(starts, chunks):
            inter = lax.dot_general(ch["q_in"], st.astype(BF16), NT_DIMS,
                                    preferred_element_type=F32)
            st = st * ch["decay"] + ch["upd"]
            o = inter + ch["intra"]
            ms = jnp.mean(o * o, axis=-1, keepdims=True)
            on = (o * lax.rsqrt(ms + RMS_EPS)) * ong_ref[...]
            gv = g_ref[0, pl.ds(r0, c64), :]
            o_ref[0, pl.ds(r0, c64), :] = (on * (gv * jax.nn.sigmoid(gv))).astype(o_ref.dtype)
        state_ref[...] = st
        return carry

    lax.fori_loop(0, ts // (per_step * c64), step, 0)


def _hgrn_core(proj, lb_logits, o_norm_g, layer, n_heads):
    bsz, s, d4 = proj.shape
    d = d4 // 4
    assert d // n_heads == LANES
    ts = _tile(s, 512)
    per_step = math.gcd(HGRN_CHUNKS_PER_STEP, ts // CHUNK)
    depth = lb_logits.shape[0]
    return pl.pallas_call(
        functools.partial(_hgrn_kernel, layer=layer, ts=ts),
        grid=(bsz, n_heads, s // ts),
        in_specs=[pl.BlockSpec((1, ts, LANES), lambda b, h, t: (b, t, h)),
                  pl.BlockSpec((1, ts, LANES), lambda b, h, t: (b, t, n_heads + h)),
                  pl.BlockSpec((1, ts, LANES), lambda b, h, t: (b, t, 2 * n_heads + h)),
                  pl.BlockSpec((1, ts, LANES), lambda b, h, t: (b, t, 3 * n_heads + h)),
                  pl.BlockSpec((depth, LANES), lambda b, h, t: (0, h)),
                  pl.BlockSpec((1, LANES), lambda b, h, t: (0, 0))],
        out_specs=pl.BlockSpec((1, ts, LANES), lambda b, h, t: (b, t, h)),
        out_shape=jax.ShapeDtypeStruct((bsz, s, d), BF16),
        scratch_shapes=[pltpu.VMEM((LANES, LANES), F32),
                        pltpu.VMEM((per_step, CHUNK, LANES), F32)],
        compiler_params=_params("parallel", "parallel", "arbitrary"),
        name="hgrn2",
    )(proj, proj, proj, proj, lb_logits, o_norm_g.reshape(1, LANES))


def kernel(x, c, ada_w, ada_b, norm_g, ffn_w_gate, ffn_w_up, ffn_w_down,
           attn_w_in, attn_w_out, attn_q_gain, attn_k_gain, attn_lambda, attn_subln_g,
           conv_w_in, conv_w, conv_w_out,
           hgrn_w_in, hgrn_w_out, hgrn_o_norm_g, hgrn_lb_logits):
    depth = ada_w.shape[0]
    d = x.shape[-1]
    qk_dim = attn_q_gain.shape[-1]
    attn_heads = d // (2 * qk_dim)
    hgrn_heads = d // hgrn_o_norm_g.shape[-1]
    ffn_w_gate, ffn_w_up, ffn_w_down = (w.astype(BF16) for w in (ffn_w_gate, ffn_w_up, ffn_w_down))
    attn_w_in, attn_w_out = attn_w_in.astype(BF16), attn_w_out.astype(BF16)
    conv_w_in, conv_w_out = conv_w_in.astype(BF16), conv_w_out.astype(BF16)
    hgrn_w_in, hgrn_w_out = hgrn_w_in.astype(BF16), hgrn_w_out.astype(BF16)

    mod = _ada_mod(c, ada_w, ada_b)
    slopes2 = jnp.asarray(
        [2.0 ** (-8.0 * (h + 1) / attn_heads) * LOG2E for h in range(attn_heads)], F32)
    q_scale = qk_dim ** -0.5 * LOG2E

    for layer in range(depth):
        x = _ffn(x, mod, norm_g, ffn_w_gate, ffn_w_up, ffn_w_down, layer, 0)
        kind, slot = layer % N_MIXERS, layer // N_MIXERS
        if kind == 0:
            lambda_init = 0.8 - 0.6 * math.exp(-0.3 * layer)
            gains = jnp.stack([attn_q_gain[slot], attn_k_gain[slot]])
            qkv = _proj_in(x, mod, norm_g, attn_w_in, layer, slot, BF16,
                           gains=gains, q_scale=q_scale)
            logit_bound = (1.02 * qk_dim * q_scale * jnp.max(jnp.abs(attn_q_gain[slot]))
                           * jnp.max(jnp.abs(attn_k_gain[slot])))
            y = _attention(qkv, slopes2, logit_bound, attn_lambda[slot], attn_subln_g[slot],
                           lambda_init, attn_heads)
            x = _proj_out(y, attn_w_out, x, mod, layer, slot)
        elif kind == 1:
            y = _conv_in(x, mod, norm_g, conv_w_in, conv_w, layer, slot)
            x = _proj_out(y, conv_w_out, x, mod, layer, slot)
        else:
            proj = _proj_in(x, mod, norm_g, hgrn_w_in, layer, slot, F32)
            y = _hgrn_core(proj, hgrn_lb_logits, hgrn_o_norm_g[slot], layer, hgrn_heads)
            x = _proj_out(y, hgrn_w_out, x, mod, layer, slot)
        x = _ffn(x, mod, norm_g, ffn_w_gate, ffn_w_up, ffn_w_down, layer, 1)
    return x
```

```python
import functools
import math

import jax
import jax.numpy as jnp
from jax import lax
from jax.experimental import pallas as pl
from jax.experimental.pallas import tpu as pltpu

F32 = jnp.float32
BF16 = jnp.bfloat16

RMS_EPS = 1e-6
CHUNK = 64
N_MIXERS = 3
N_MOD = 9
CONV_WIDTH = 3
LOG2E = 1.4426950408889634
NEG_BIG = -1e30
F32_MIN_EXP = 127
F32_MANT_BITS = 24

LANES = 128
SUBLANES = 8
BF16_SUBLANES = 16
NORM_ROW_BLOCK = 16
V7X_VMEM_LIMIT_BYTES = 60 * 1024 * 1024

NT_DIMS = (((1,), (1,)), ((), ()))
TN_DIMS = (((0,), (0,)), ((), ()))


def _tile(n, preferred):
    for t in (preferred, 2048, 1024, 512, 256, 128, 64, 32, 16, 8):
        if t <= preferred and n % t == 0:
            return t
    return n


def _params(*sem):
    return pltpu.CompilerParams(dimension_semantics=sem,
                                vmem_limit_bytes=V7X_VMEM_LIMIT_BYTES)


def _norm_modulate_into(h_ref, x_ref, g, shift, scale):
    gain = g * (1.0 + scale)
    rows = h_ref.shape[0]
    blk = min(rows, NORM_ROW_BLOCK)
    for r in range(0, rows, blk):
        x = x_ref[0, r:r + blk, :]
        ms = jnp.mean(x * x, axis=-1, keepdims=True)
        h_ref[r:r + blk, :] = ((x * lax.rsqrt(ms + RMS_EPS)) * gain + shift).astype(BF16)


def _split3(x):
    hi = x.astype(BF16)
    r1 = x - hi.astype(F32)
    mid = r1.astype(BF16)
    lo = (r1 - mid.astype(F32)).astype(BF16)
    return hi, mid, lo


def _ada_kernel(c_ref, w_ref, b_ref, o_ref):
    c = c_ref[...]
    cond = (c * jax.nn.sigmoid(c)).astype(BF16)
    o_ref[0] = jnp.dot(cond, w_ref[0].astype(BF16),
                       preferred_element_type=F32) + b_ref[0]


def _ada_mod(c, ada_w, ada_b):
    depth, d, n = ada_w.shape
    bsz = c.shape[0]
    rows = -(-bsz // SUBLANES) * SUBLANES
    c_pad = jnp.zeros((rows, d), F32).at[:bsz].set(c)
    tn = _tile(n, 1024)
    out = pl.pallas_call(
        _ada_kernel,
        grid=(depth, n // tn),
        in_specs=[pl.BlockSpec((rows, d), lambda l, j: (0, 0)),
                  pl.BlockSpec((1, d, tn), lambda l, j: (l, 0, j)),
                  pl.BlockSpec((1, 1, tn), lambda l, j: (l, 0, j))],
        out_specs=pl.BlockSpec((1, rows, tn), lambda l, j: (l, 0, j)),
        out_shape=jax.ShapeDtypeStruct((depth, rows, n), F32),
        compiler_params=_params("parallel", "parallel"),
        name="ada_mod",
    )(c_pad, ada_w, ada_b.reshape(depth, 1, n))
    return out[:, :bsz].reshape(depth, bsz, N_MOD, d)


FFN_ROW_TILE = 1024
FFN_COL_TILE = 512


def _ffn_kernel(x_ref, mod_ref, g_ref, wg_ref, wu_ref, wd_ref, o_ref, h_ref, *, sub, row_groups):
    f = pl.program_id(2)
    r = 3 * sub

    @pl.when(f == 0)
    def _():
        _norm_modulate_into(h_ref, x_ref, g_ref[0, sub:sub + 1, :],
                            mod_ref[0, 0, r:r + 1, :], mod_ref[0, 0, r + 1:r + 2, :])
        o_ref[0] = jnp.zeros(o_ref.shape[1:], F32)

    rows = h_ref.shape[0] // row_groups
    slices = [slice(grp * rows, (grp + 1) * rows) for grp in range(row_groups)]
    gated = []
    for sl in slices:
        h = h_ref[sl, :]
        a = jnp.dot(h, wg_ref[0, 0], preferred_element_type=F32)
        b = jnp.dot(h, wu_ref[0, 0], preferred_element_type=F32)
        gated.append((a * jax.nn.sigmoid(a) * b).astype(BF16))
    for sl, p in zip(slices, gated):
        o_ref[0, sl, :] += jnp.dot(p, wd_ref[0, 0], preferred_element_type=F32)

    @pl.when(f == pl.num_programs(2) - 1)
    def _():
        o_ref[0] = x_ref[0] + (0.5 * mod_ref[0, 0, r + 2:r + 3, :]) * o_ref[0]


def _ffn(x, mod, norm_g, w_gate, w_up, w_down, layer, which):
    bsz, s, d = x.shape
    f = w_gate.shape[-1]
    tm = _tile(s, FFN_ROW_TILE)
    tf = _tile(f, FFN_COL_TILE)
    return pl.pallas_call(
        functools.partial(_ffn_kernel, sub=2 * which, row_groups=2 if tm % 256 == 0 else 1),
        grid=(bsz, s // tm, f // tf),
        in_specs=[pl.BlockSpec((1, tm, d), lambda b, i, j: (b, i, 0)),
                  pl.BlockSpec((1, 1, N_MOD, d), lambda b, i, j: (layer, b, 0, 0)),
                  pl.BlockSpec((1, 3, d), lambda b, i, j: (layer, 0, 0)),
                  pl.BlockSpec((1, 1, d, tf), lambda b, i, j: (layer, which, 0, j)),
                  pl.BlockSpec((1, 1, d, tf), lambda b, i, j: (layer, which, 0, j)),
                  pl.BlockSpec((1, 1, tf, d), lambda b, i, j: (layer, which, j, 0))],
        out_specs=pl.BlockSpec((1, tm, d), lambda b, i, j: (b, i, 0)),
        out_shape=jax.ShapeDtypeStruct((bsz, s, d), F32),
        scratch_shapes=[pltpu.VMEM((tm, d), BF16)],
        compiler_params=_params("parallel", "parallel", "arbitrary"),
        name="ffn",
    )(x, mod, norm_g, w_gate, w_up, w_down)


def _qk_norm_store(acc, gain, group_mean, o_ref, post_scale):
    width = group_mean.shape[0]
    if post_scale != 1.0:
        gain = gain * post_scale
    for j in range(acc.shape[1] // width):
        sl = slice(j * width, (j + 1) * width)
        y = acc[:, sl]
        y2 = y * y
        y2_hi = y2.astype(BF16)
        y2_lo = (y2 - y2_hi.astype(F32)).astype(BF16)
        ms = (jnp.dot(y2_hi, group_mean, preferred_element_type=F32)
              + jnp.dot(y2_lo, group_mean, preferred_element_type=F32))
        yn = (y * lax.rsqrt(ms + RMS_EPS)) * gain[:, sl]
        o_ref[0, :, sl] = yn.astype(o_ref.dtype)


def _proj_in_kernel(x_ref, mod_ref, g_ref, w_ref, *rest, qk_norm, q_scale, tiles_per_section):
    if qk_norm:
        gains_ref, gmean_ref, o_ref, h_ref = rest
    else:
        o_ref, h_ref = rest
    n = pl.program_id(2)

    @pl.when(n == 0)
    def _():
        _norm_modulate_into(h_ref, x_ref, g_ref[0, 1:2, :],
                            mod_ref[0, 0, 3:4, :], mod_ref[0, 0, 4:5, :])

    acc = jnp.dot(h_ref[...], w_ref[0], preferred_element_type=F32)
    if not qk_norm:
        o_ref[0] = acc.astype(o_ref.dtype)
        return

    section = n // tiles_per_section

    @pl.when(section == 0)
    def _():
        _qk_norm_store(acc, gains_ref[0:1, :], gmean_ref[...], o_ref, q_scale)

    @pl.when(section == 1)
    def _():
        _qk_norm_store(acc, gains_ref[1:2, :], gmean_ref[...], o_ref, 1.0)

    @pl.when(section == 2)
    def _():
        o_ref[0] = acc.astype(o_ref.dtype)


def _proj_in(x, mod, norm_g, w, layer, slot, out_dtype, gains=None, q_scale=1.0):
    bsz, s, d = x.shape
    n = w.shape[-1]
    tm = _tile(s, 1024)
    tn = _tile(d, 1024)
    qk_norm = gains is not None
    in_specs = [pl.BlockSpec((1, tm, d), lambda b, i, j: (b, i, 0)),
                pl.BlockSpec((1, 1, N_MOD, d), lambda b, i, j: (layer, b, 0, 0)),
                pl.BlockSpec((1, 3, d), lambda b, i, j: (layer, 0, 0)),
                pl.BlockSpec((1, d, tn), lambda b, i, j: (slot, 0, j))]
    args = [x, mod, norm_g, w]
    if qk_norm:
        group = gains.shape[1]
        width = _tile(tn, 2 * LANES)
        blk = jnp.arange(width) // group
        gmean = jnp.where(blk[:, None] == blk[None, :], 1.0 / group, 0.0).astype(BF16)
        in_specs += [pl.BlockSpec((2, tn), lambda b, i, j: (0, 0)),
                     pl.BlockSpec((width, width), lambda b, i, j: (0, 0))]
        args += [jnp.tile(gains, (1, tn // group)), gmean]
    return pl.pallas_call(
        functools.partial(_proj_in_kernel, qk_norm=qk_norm, q_scale=q_scale,
                          tiles_per_section=d // tn),
        grid=(bsz, s // tm, n // tn),
        in_specs=in_specs,
        out_specs=pl.BlockSpec((1, tm, tn), lambda b, i, j: (b, i, j)),
        out_shape=jax.ShapeDtypeStruct((bsz, s, n), out_dtype),
        scratch_shapes=[pltpu.VMEM((tm, d), BF16)],
        compiler_params=_params("parallel", "parallel", "arbitrary"),
        name="proj_in",
    )(*args)


def _proj_out_kernel(y_ref, w_ref, x_ref, mod_ref, o_ref):
    o_ref[0] = x_ref[0] + mod_ref[0, 0, 5:6, :] * jnp.dot(
        y_ref[0], w_ref[0], preferred_element_type=F32)


def _proj_out(y, w, x, mod, layer, slot):
    bsz, s, d = x.shape
    k = y.shape[2]
    tm = _tile(s, 512)
    return pl.pallas_call(
        _proj_out_kernel,
        grid=(bsz, s // tm),
        in_specs=[pl.BlockSpec((1, tm, k), lambda b, i: (b, i, 0)),
                  pl.BlockSpec((1, k, d), lambda b, i: (slot, 0, 0)),
                  pl.BlockSpec((1, tm, d), lambda b, i: (b, i, 0)),
                  pl.BlockSpec((1, 1, N_MOD, d), lambda b, i: (layer, b, 0, 0))],
        out_specs=pl.BlockSpec((1, tm, d), lambda b, i: (b, i, 0)),
        out_shape=jax.ShapeDtypeStruct((bsz, s, d), F32),
        compiler_params=_params("parallel", "parallel"),
        name="proj_out",
    )(y, w, x, mod)


ACC_ROWS = LANES + BF16_SUBLANES
N_BIAS_COLS = 6
ATTN_TILE = 512
CARRY_SLOT = 2


def _attn_kernel(skip_ref, q_ref, k_ref, v_ref, slope_ref, qbias_ref, lam_ref, subg_ref,
                 dist_ref, mask_ref, o_ref, kaug_ref, vt_ref, own_ref, s_ref, mt_ref, m_ref,
                 acc_ref, *, tq, tk, lambda_init):
    h = pl.program_id(1)
    n_kv = vt_ref.shape[0]
    n_q = q_ref.shape[1] // tq

    own_ref[...] = slope_ref[0] * dist_ref[...] + mask_ref[...]
    row16 = lax.broadcasted_iota(jnp.int32, (BF16_SUBLANES, tk), 0)
    ones_rows = jnp.where(row16 == 0, 1.0, 0.0).astype(BF16)
    lane = lax.broadcasted_iota(jnp.int32, (tk, LANES), 1)
    row = lax.broadcasted_iota(jnp.int32, (tk, LANES), 0)
    pos = jnp.where(lane < N_BIAS_COLS // 2, row >> 1,
                    jnp.where(lane < N_BIAS_COLS, row & 1, 0)).astype(F32).astype(BF16)
    for c in range(n_kv):
        blk = v_ref[0, c * tk:(c + 1) * tk, :].astype(F32)
        vt_ref[c, 0:LANES, :] = blk.T.astype(BF16)
        vt_ref[c, LANES:ACC_ROWS, :] = ones_rows
        kaug_ref[c, :, 0:LANES] = k_ref[0, c * tk:(c + 1) * tk, :]
        kaug_ref[c, :, LANES:2 * LANES] = pos

    lo = lax.broadcasted_iota(jnp.int32, (1, LANES), 1) < LANES // 2
    q_bias = jnp.broadcast_to(qbias_ref[0], (tq, LANES)).astype(BF16)
    slope2 = slope_ref[0]

    def augmented_queries(qt):
        q = q_ref[0, pl.ds(pl.multiple_of(qt * tq, tq), tq), :]
        zero = jnp.zeros_like(q)
        return (jnp.concatenate([jnp.where(lo, q, zero), q_bias], axis=1),
                jnp.concatenate([jnp.where(lo, zero, q), q_bias], axis=1))

    def first_tile(qt):
        return jnp.maximum(qt * tq - skip_ref[h] + 1, 0) // tk

    def stage_a(j, slot, q_pair):
        kt = kaug_ref[j]
        for mp in range(2):
            s = lax.dot_general(kt, q_pair[mp], NT_DIMS, preferred_element_type=F32)
            s_ref[slot, mp] = s
            mt_ref[slot, mp] = jnp.max(s, axis=0, keepdims=True)

    def accumulate(mp, z, tile_max, shift, vt):
        m_old = m_ref[mp]
        m_new = jnp.maximum(m_old, tile_max + shift)
        alpha = jnp.exp2(m_old - m_new)
        p = jnp.exp2(z - (m_new - shift)).astype(BF16)
        acc_ref[mp] = alpha * acc_ref[mp] + jnp.dot(vt, p, preferred_element_type=F32)
        m_ref[mp] = m_new

    stage_a(0, CARRY_SLOT, augmented_queries(0))

    def query_tile(qi, carry):
        q_aug = augmented_queries(qi)
        acc_ref[...] = jnp.zeros_like(acc_ref)
        m_ref[...] = jnp.full(m_ref.shape, NEG_BIG, F32)

        def stage_b(j, slot):
            shift = slope2 * (j * tk - qi * tq).astype(F32)
            vt = vt_ref[j]
            for mp in range(2):
                accumulate(mp, s_ref[slot, mp], mt_ref[slot, mp], shift, vt)

        def stage_b_own(slot):
            vt = vt_ref[qi]
            for mp in range(2):
                z = s_ref[slot, mp] + own_ref[...]
                accumulate(mp, z, jnp.max(z, axis=0, keepdims=True), 0.0, vt)

        def finish():
            qn = jnp.minimum(qi + 1, n_q - 1)
            stage_a(first_tile(qn), CARRY_SLOT, augmented_queries(qn))
            lam_v = lam_ref[...]
            lam = (jnp.exp(jnp.sum(lam_v[0:1] * lam_v[1:2], axis=-1, keepdims=True))
                   - jnp.exp(jnp.sum(lam_v[2:3] * lam_v[3:4], axis=-1, keepdims=True))
                   + lambda_init)
            o = (acc_ref[0, 0:LANES, :] / acc_ref[0, LANES:LANES + 1, :]
                 - lam * (acc_ref[1, 0:LANES, :] / acc_ref[1, LANES:LANES + 1, :]))
            ms = jnp.mean(o * o, axis=0, keepdims=True)
            on = (o * lax.rsqrt(ms + RMS_EPS)).T
            o_ref[0, pl.ds(pl.multiple_of(qi * tq, tq), tq), :] = (
                (on * subg_ref[...]) * (1.0 - lambda_init)).astype(o_ref.dtype)

        j0 = first_tile(qi)
        count = qi - j0
        rest = jnp.maximum(count - 1, 0)

        @pl.when(count == 0)
        def _():
            stage_b_own(CARRY_SLOT)
            finish()

        @pl.when(count > 0)
        def _():
            stage_a(j0 + 1, 0, q_aug)
            stage_b(j0, CARRY_SLOT)

        def pair_at(j):
            stage_a(j + 1, 1, q_aug)
            stage_b(j, 0)
            stage_a(j + 2, 0, q_aug)
            stage_b(j + 1, 1)

        def quad(t, c):
            j = j0 + 1 + 4 * t
            pair_at(j)
            pair_at(j + 2)
            return c

        def pair(t, c):
            pair_at(j0 + 1 + 4 * (rest // 4) + 2 * t)
            return c

        lax.fori_loop(0, rest // 4, quad, 0)
        lax.fori_loop(0, (rest % 4) // 2, pair, 0)

        @pl.when((count > 0) & (rest % 2 == 0))
        def _():
            stage_b_own(0)
            finish()

        @pl.when((count > 0) & (rest % 2 == 1))
        def _():
            stage_a(qi, 1, q_aug)
            stage_b(qi - 1, 0)
            stage_b_own(1)
            finish()

        return carry

    lax.fori_loop(0, n_q, query_tile, 0)


def _attention(qkv, slopes2, logit_bound, lam_vecs, subln_g, lambda_init, n_heads):
    bsz, s, d3 = qkv.shape
    d = d3 // 3
    assert d // n_heads == LANES
    tq = _tile(s, ATTN_TILE)
    tk = tq
    assert tk <= 2 * 256

    r = jnp.arange(tk)[:, None]
    c = jnp.arange(tq)[None, :]
    dist = jnp.minimum(0, 2 * (c - r)).astype(F32)
    mask = jnp.where(r // CHUNK <= c // CHUNK, 0.0, NEG_BIG).astype(F32)

    slope_rows = jnp.broadcast_to(slopes2[:, None, None], (n_heads, 1, tq)).astype(F32)
    hi, mid, lo = (p.astype(F32) for p in _split3(slopes2))
    pieces = jnp.stack([2 * hi, 2 * mid, 2 * lo, hi, mid, lo], axis=-1)
    qbias = jnp.zeros((n_heads, 1, LANES), F32).at[:, 0, :N_BIAS_COLS].set(pieces)
    skip = jnp.ceil((F32_MIN_EXP + F32_MANT_BITS + 2.0 * logit_bound) / slopes2)
    skip = jnp.clip(skip, 1, 2 ** 30).astype(jnp.int32)

    grid_spec = pltpu.PrefetchScalarGridSpec(
        num_scalar_prefetch=1,
        grid=(bsz, n_heads),
        in_specs=[pl.BlockSpec((1, s, LANES), lambda b, h, sk: (b, 0, h)),
                  pl.BlockSpec((1, s, LANES), lambda b, h, sk: (b, 0, n_heads + h)),
                  pl.BlockSpec((1, s, LANES), lambda b, h, sk: (b, 0, 2 * n_heads + h)),
                  pl.BlockSpec((1, 1, tq), lambda b, h, sk: (h, 0, 0)),
                  pl.BlockSpec((1, 1, LANES), lambda b, h, sk: (h, 0, 0)),
                  pl.BlockSpec(lam_vecs.shape, lambda b, h, sk: (0, 0)),
                  pl.BlockSpec((1, LANES), lambda b, h, sk: (0, 0)),
                  pl.BlockSpec((tk, tq), lambda b, h, sk: (0, 0)),
                  pl.BlockSpec((tk, tq), lambda b, h, sk: (0, 0))],
        out_specs=pl.BlockSpec((1, s, LANES), lambda b, h, sk: (b, 0, h)),
        scratch_shapes=[pltpu.VMEM((s // tk, tk, 2 * LANES), BF16),
                        pltpu.VMEM((s // tk, ACC_ROWS, tk), BF16),
                        pltpu.VMEM((tk, tq), F32),
                        pltpu.VMEM((3, 2, tk, tq), F32),
                        pltpu.VMEM((3, 2, 1, tq), F32),
                        pltpu.VMEM((2, 1, tq), F32),
                        pltpu.VMEM((2, ACC_ROWS, tq), F32)])
    return pl.pallas_call(
        functools.partial(_attn_kernel, tq=tq, tk=tk, lambda_init=lambda_init),
        grid_spec=grid_spec,
        out_shape=jax.ShapeDtypeStruct((bsz, s, d), BF16),
        compiler_params=_params("parallel", "parallel"),
        name="diff_attn",
    )(skip, qkv, qkv, qkv, slope_rows, qbias, lam_vecs, subln_g.reshape(1, LANES), dist, mask)


CONV_HALO = BF16_SUBLANES


def _conv_in_kernel(x_ref, xh_ref, mod_ref, g_ref, wb_ref, wc_ref, wu_ref, cw_ref, o_ref,
                    h_ref, ext_ref):
    i = pl.program_id(1)
    n = pl.program_id(2)
    tm = x_ref.shape[1]
    halo = CONV_HALO

    @pl.when(n == 0)
    def _():
        g, shift, scale = g_ref[0, 1:2, :], mod_ref[0, 0, 3:4, :], mod_ref[0, 0, 4:5, :]
        _norm_modulate_into(h_ref.at[0:halo], xh_ref, g, shift, scale)
        _norm_modulate_into(h_ref.at[halo:halo + tm], x_ref, g, shift, scale)

    h_all = h_ref[...]
    c = jnp.dot(h_all, wc_ref[0], preferred_element_type=F32)
    u = jnp.dot(h_all, wu_ref[0], preferred_element_type=F32)
    b = jnp.dot(h_ref[halo:halo + tm, :], wb_ref[0], preferred_element_type=F32)
    v = c * u
    ext_ref[0:halo, :] = jnp.where(i == 0, 0.0, v[0:halo])
    ext_ref[halo:, :] = v[halo:]
    v1 = ext_ref[halo - 1:halo - 1 + tm, :]
    v2 = ext_ref[halo - 2:halo - 2 + tm, :]
    w = cw_ref[0]
    y = w[0:1] * v2 + w[1:2] * v1 + w[2:3] * v[halo:]
    o_ref[0] = (b * y).astype(o_ref.dtype)


def _conv_in(x, mod, norm_g, w_in, conv_w, layer, slot):
    bsz, s, d = x.shape
    assert CONV_WIDTH - 1 <= CONV_HALO
    tm = _tile(s, 1024)
    tn = _tile(d, 512)
    nc = d // tn
    hb = tm // CONV_HALO
    return pl.pallas_call(
        _conv_in_kernel,
        grid=(bsz, s // tm, nc),
        in_specs=[pl.BlockSpec((1, tm, d), lambda b, i, j: (b, i, 0)),
                  pl.BlockSpec((1, CONV_HALO, d),
                               lambda b, i, j: (b, jnp.maximum(i * hb - 1, 0), 0)),
                  pl.BlockSpec((1, 1, N_MOD, d), lambda b, i, j: (layer, b, 0, 0)),
                  pl.BlockSpec((1, 3, d), lambda b, i, j: (layer, 0, 0)),
                  pl.BlockSpec((1, d, tn), lambda b, i, j: (slot, 0, j)),
                  pl.BlockSpec((1, d, tn), lambda b, i, j: (slot, 0, nc + j)),
                  pl.BlockSpec((1, d, tn), lambda b, i, j: (slot, 0, 2 * nc + j)),
                  pl.BlockSpec((1, CONV_WIDTH, tn), lambda b, i, j: (slot, 0, j))],
        out_specs=pl.BlockSpec((1, tm, tn), lambda b, i, j: (b, i, j)),
        out_shape=jax.ShapeDtypeStruct((bsz, s, d), BF16),
        scratch_shapes=[pltpu.VMEM((CONV_HALO + tm, d), BF16),
                        pltpu.VMEM((CONV_HALO + tm, tn), F32)],
        compiler_params=_params("parallel", "parallel", "arbitrary"),
        name="conv_in",
    )(x, x, mod, norm_g, w_in, w_in, w_in, conv_w)


HGRN_CHUNKS_PER_STEP = 8
HGRN_TIME_TILE = 8192


def _hgrn_kernel(q_ref, fl_ref, i_ref, g_ref, lbl_ref, ong_ref, o_ref,
                 state_ref, b_ref, *, layer, ts):
    t = pl.program_id(2)
    c64 = CHUNK
    sb = SUBLANES

    @pl.when(t == 0)
    def _():
        state_ref[...] = jnp.zeros_like(state_ref)

    lbl = lbl_ref[...]
    e = jnp.exp(lbl - jnp.max(lbl, axis=0, keepdims=True))
    sm = e / jnp.sum(e, axis=0, keepdims=True)
    lb = jnp.zeros((1, LANES), F32)
    for r in range(1, layer + 1):
        lb = lb + sm[r:r + 1]

    row = lax.broadcasted_iota(jnp.int32, (c64, c64), 0)
    col = lax.broadcasted_iota(jnp.int32, (c64, c64), 1)
    tri = jnp.where(row >= col, 1.0, 0.0).astype(BF16)
    ones = jnp.ones((LANES, LANES), BF16)
    sub_row = lax.broadcasted_iota(jnp.int32, (sb, LANES), 0)
    col8 = lax.broadcasted_iota(jnp.int32, (sb, c64), 1)
    groups = (2 * sb, 4 * sb, 8 * sb)
    level_masks = [(row // grp == col // grp) & (row % grp >= grp // 2) & (col % grp < grp // 2)
                   for grp in groups]


    def gates_and_cumsum(r0, u):
        fl = fl_ref[0, pl.ds(r0, c64), :]
        f = lb + (1.0 - lb) * jax.nn.sigmoid(fl)
        lf = jnp.log2(f)
        b = jnp.zeros((c64, LANES), F32)
        for piece in _split3(lf):
            b = b + jnp.dot(tri, piece, preferred_element_type=F32)
        b_ref[u] = b
        return dict(q=q_ref[0, pl.ds(r0, c64), :], kk=1.0 - f, b=b,
                    i16=i_ref[0, pl.ds(r0, c64), :].astype(BF16))

    def diagonal_sums(ch, u):
        w_rows = []
        for blk in range(c64 // sb):
            b_blk = ch["b"][blk * sb:(blk + 1) * sb]
            q_blk = ch["q"][blk * sb:(blk + 1) * sb]
            for s_ in range(sb):
                src = blk * sb + s_
                dec = jnp.exp2(b_blk - b_ref[u, src:src + 1, :])
                w_rows.append(jnp.where(sub_row >= s_,
                                        q_blk * dec * ch["kk"][src:src + 1, :], 0.0))
        w_all = jnp.concatenate(w_rows, axis=0).astype(BF16)
        ch["sums"] = jnp.dot(w_all, ones, preferred_element_type=F32)

    def level_scores(ch, u):
        q, kk, b = ch["q"], ch["kk"], ch["b"]
        out = []
        for grp in groups:
            parts = []
            for gidx in range(c64 // grp):
                edge = gidx * grp + grp // 2 - 1
                parts.append(jnp.broadcast_to(b_ref[u, edge:edge + 1, :], (grp, LANES)))
            ref = parts[0] if len(parts) == 1 else jnp.concatenate(parts, axis=0)
            qd = (q * jnp.exp2(jnp.minimum(b - ref, 0.0))).astype(BF16)
            kd = (kk * jnp.exp2(jnp.minimum(ref - b, 0.0))).astype(BF16)
            out.append(lax.dot_general(qd, kd, NT_DIMS, preferred_element_type=F32))
        ch["levels"] = out
        b_last = b_ref[u, c64 - 1:c64, :]
        kd_end = (kk * jnp.exp2(b_last - b)).astype(BF16)
        ch["upd"] = lax.dot_general(ch["i16"], kd_end, TN_DIMS,
                                    preferred_element_type=F32)
        ch["decay"] = jnp.exp2(b_last)
        ch["q_in"] = (q * jnp.exp2(b)).astype(BF16)

    def intra_chunk(ch):
        score_rows = []
        for blk in range(c64 // sb):
            acc = jnp.zeros((sb, c64), F32)
            for s_ in range(sb):
                idx = blk * sb + s_
                acc = jnp.where(col8 == idx, ch["sums"][idx * sb:(idx + 1) * sb, :c64], acc)
            score_rows.append(acc)
        scores = jnp.concatenate(score_rows, axis=0)
        for lvl in range(len(groups)):
            scores = scores + jnp.where(level_masks[lvl], ch["levels"][lvl], 0.0)
        ch["intra"] = jnp.dot(scores.astype(BF16), ch["i16"], preferred_element_type=F32)

    per_step = b_ref.shape[0]

    def step(c, carry):
        base = c * (per_step * c64)
        starts = [pl.multiple_of(base + u * c64, c64) for u in range(per_step)]
        chunks = [gates_and_cumsum(r0, u) for u, r0 in enumerate(starts)]
        for u, ch in enumerate(chunks):
            diagonal_sums(ch, u)
        for u, ch in enumerate(chunks):
            level_scores(ch, u)
        for ch in chunks:
            intra_chunk(ch)
        st = state_ref[...]
        for r0, ch in zip(starts, chunks):
            inter = lax.dot_general(ch["q_in"], st.astype(BF16), NT_DIMS,
                                    preferred_element_type=F32)
            st = st * ch["decay"] + ch["upd"]
            o = inter + ch["intra"]
            ms = jnp.mean(o * o, axis=-1, keepdims=True)
            on = (o * lax.rsqrt(ms + RMS_EPS)) * ong_ref[...]
            gv = g_ref[0, pl.ds(r0, c64), :]
            o_ref[0, pl.ds(r0, c64), :] = (on * (gv * jax.nn.sigmoid(gv))).astype(o_ref.dtype)
        state_ref[...] = st
        return carry

    lax.fori_loop(0, ts // (per_step * c64), step, 0)


def _hgrn_core(proj, lb_logits, o_norm_g, layer, n_heads):
    bsz, s, d4 = proj.shape
    d = d4 // 4
    assert d // n_heads == LANES
    ts = _tile(s, HGRN_TIME_TILE)
    per_step = math.gcd(HGRN_CHUNKS_PER_STEP, ts // CHUNK)
    depth = lb_logits.shape[0]
    return pl.pallas_call(
        functools.partial(_hgrn_kernel, layer=layer, ts=ts),
        grid=(bsz, n_heads, s // ts),
        in_specs=[pl.BlockSpec((1, ts, LANES), lambda b, h, t: (b, t, h)),
                  pl.BlockSpec((1, ts, LANES), lambda b, h, t: (b, t, n_heads + h)),
                  pl.BlockSpec((1, ts, LANES), lambda b, h, t: (b, t, 2 * n_heads + h)),
                  pl.BlockSpec((1, ts, LANES), lambda b, h, t: (b, t, 3 * n_heads + h)),
                  pl.BlockSpec((depth, LANES), lambda b, h, t: (0, h)),
                  pl.BlockSpec((1, LANES), lambda b, h, t: (0, 0))],
        out_specs=pl.BlockSpec((1, ts, LANES), lambda b, h, t: (b, t, h)),
        out_shape=jax.ShapeDtypeStruct((bsz, s, d), BF16),
        scratch_shapes=[pltpu.VMEM((LANES, LANES), F32),
                        pltpu.VMEM((per_step, CHUNK, LANES), F32)],
        compiler_params=_params("parallel", "parallel", "arbitrary"),
        name="hgrn2",
    )(proj, proj, proj, proj, lb_logits, o_norm_g.reshape(1, LANES))


def kernel(x, c, ada_w, ada_b, norm_g, ffn_w_gate, ffn_w_up, ffn_w_down,
           attn_w_in, attn_w_out, attn_q_gain, attn_k_gain, attn_lambda, attn_subln_g,
           conv_w_in, conv_w, conv_w_out,
           hgrn_w_in, hgrn_w_out, hgrn_o_norm_g, hgrn_lb_logits):
    depth = ada_w.shape[0]
    d = x.shape[-1]
    qk_dim = attn_q_gain.shape[-1]
    attn_heads = d // (2 * qk_dim)
    hgrn_heads = d // hgrn_o_norm_g.shape[-1]
    ffn_w_gate, ffn_w_up, ffn_w_down = (w.astype(BF16) for w in (ffn_w_gate, ffn_w_up, ffn_w_down))
    attn_w_in, attn_w_out = attn_w_in.astype(BF16), attn_w_out.astype(BF16)
    conv_w_in, conv_w_out = conv_w_in.astype(BF16), conv_w_out.astype(BF16)
    hgrn_w_in, hgrn_w_out = hgrn_w_in.astype(BF16), hgrn_w_out.astype(BF16)

    mod = _ada_mod(c, ada_w, ada_b)
    slopes2 = jnp.asarray(
        [2.0 ** (-8.0 * (h + 1) / attn_heads) * LOG2E for h in range(attn_heads)], F32)
    q_scale = qk_dim ** -0.5 * LOG2E

    for layer in range(depth):
        x = _ffn(x, mod, norm_g, ffn_w_gate, ffn_w_up, ffn_w_down, layer, 0)
        kind, slot = layer % N_MIXERS, layer // N_MIXERS
        if kind == 0:
            lambda_init = 0.8 - 0.6 * math.exp(-0.3 * layer)
            gains = jnp.stack([attn_q_gain[slot], attn_k_gain[slot]])
            qkv = _proj_in(x, mod, norm_g, attn_w_in, layer, slot, BF16,
                           gains=gains, q_scale=q_scale)
            logit_bound = (1.02 * qk_dim * q_scale * jnp.max(jnp.abs(attn_q_gain[slot]))
                           * jnp.max(jnp.abs(attn_k_gain[slot])))
            y = _attention(qkv, slopes2, logit_bound, attn_lambda[slot], attn_subln_g[slot],
                           lambda_init, attn_heads)
            x = _proj_out(y, attn_w_out, x, mod, layer, slot)
        elif kind == 1:
            y = _conv_in(x, mod, norm_g, conv_w_in, conv_w, layer, slot)
            x = _proj_out(y, conv_w_out, x, mod, layer, slot)
        else:
            proj = _proj_in(x, mod, norm_g, hgrn_w_in, layer, slot, F32)
            y = _hgrn_core(proj, hgrn_lb_logits, hgrn_o_norm_g[slot], layer, hgrn_heads)
            x = _proj_out(y, hgrn_w_out, x, mod, layer, slot)
        x = _ffn(x, mod, norm_g, ffn_w_gate, ffn_w_up, ffn_w_down, layer, 1)
    return x
```

```python
import functools
import math

import jax
import jax.numpy as jnp
from jax import lax
from jax.experimental import pallas as pl
from jax.experimental.pallas import tpu as pltpu

F32 = jnp.float32
BF16 = jnp.bfloat16

RMS_EPS = 1e-6
CHUNK = 64
N_MIXERS = 3
N_MOD = 9
CONV_WIDTH = 3
LOG2E = 1.4426950408889634
NEG_BIG = -1e30
F32_MIN_EXP = 127
F32_MANT_BITS = 24

LANES = 128
SUBLANES = 8
BF16_SUBLANES = 16
NORM_ROW_BLOCK = 16
V7X_VMEM_LIMIT_BYTES = 60 * 1024 * 1024

NT_DIMS = (((1,), (1,)), ((), ()))
TN_DIMS = (((0,), (0,)), ((), ()))


def _tile(n, preferred):
    for t in (preferred, 2048, 1024, 512, 256, 128, 64, 32, 16, 8):
        if t <= preferred and n % t == 0:
            return t
    return n


def _params(*sem):
    return pltpu.CompilerParams(dimension_semantics=sem,
                                vmem_limit_bytes=V7X_VMEM_LIMIT_BYTES)


def _norm_modulate_into(h_ref, x_ref, g, shift, scale):
    gain = g * (1.0 + scale)
    rows = h_ref.shape[0]
    blk = min(rows, NORM_ROW_BLOCK)
    for r in range(0, rows, blk):
        x = x_ref[0, r:r + blk, :]
        ms = jnp.mean(x * x, axis=-1, keepdims=True)
        h_ref[r:r + blk, :] = ((x * lax.rsqrt(ms + RMS_EPS)) * gain + shift).astype(BF16)


def _split3(x):
    hi = x.astype(BF16)
    r1 = x - hi.astype(F32)
    mid = r1.astype(BF16)
    lo = (r1 - mid.astype(F32)).astype(BF16)
    return hi, mid, lo


def _ada_kernel(c_ref, w_ref, b_ref, o_ref):
    c = c_ref[...]
    cond = (c * jax.nn.sigmoid(c)).astype(BF16)
    o_ref[0] = jnp.dot(cond, w_ref[0].astype(BF16),
                       preferred_element_type=F32) + b_ref[0]


def _ada_mod(c, ada_w, ada_b):
    depth, d, n = ada_w.shape
    bsz = c.shape[0]
    rows = -(-bsz // SUBLANES) * SUBLANES
    c_pad = jnp.zeros((rows, d), F32).at[:bsz].set(c)
    tn = _tile(n, 1024)
    out = pl.pallas_call(
        _ada_kernel,
        grid=(depth, n // tn),
        in_specs=[pl.BlockSpec((rows, d), lambda l, j: (0, 0)),
                  pl.BlockSpec((1, d, tn), lambda l, j: (l, 0, j)),
                  pl.BlockSpec((1, 1, tn), lambda l, j: (l, 0, j))],
        out_specs=pl.BlockSpec((1, rows, tn), lambda l, j: (l, 0, j)),
        out_shape=jax.ShapeDtypeStruct((depth, rows, n), F32),
        compiler_params=_params("parallel", "parallel"),
        name="ada_mod",
    )(c_pad, ada_w, ada_b.reshape(depth, 1, n))
    return out[:, :bsz].reshape(depth, bsz, N_MOD, d)


FFN_ROW_TILE = 1024
FFN_COL_TILE = 512


def _ffn_kernel(x_ref, mod_ref, g_ref, wg_ref, wu_ref, wd_ref, o_ref, h_ref, *, sub, row_groups,
                n_f):
    f = pl.program_id(2)
    r = 3 * sub

    @pl.when(f == 0)
    def _():
        _norm_modulate_into(h_ref, x_ref, g_ref[0, sub:sub + 1, :],
                            mod_ref[0, 0, r:r + 1, :], mod_ref[0, 0, r + 1:r + 2, :])

    rows = h_ref.shape[0] // row_groups
    slices = [slice(grp * rows, (grp + 1) * rows) for grp in range(row_groups)]

    def step(first, last):
        gated = []
        for sl in slices:
            h = h_ref[sl, :]
            a = jnp.dot(h, wg_ref[0, 0], preferred_element_type=F32)
            b = jnp.dot(h, wu_ref[0, 0], preferred_element_type=F32)
            gated.append((a * jax.nn.sigmoid(a) * b).astype(BF16))
        for sl, p in zip(slices, gated):
            total = jnp.dot(p, wd_ref[0, 0], preferred_element_type=F32)
            if not first:
                total = o_ref[0, sl, :] + total
            if last:
                total = x_ref[0, sl, :] + (0.5 * mod_ref[0, 0, r + 2:r + 3, :]) * total
            o_ref[0, sl, :] = total

    if n_f == 1:
        step(True, True)
    else:
        pl.when(f == 0)(lambda: step(True, False))
        if n_f > 2:
            pl.when((f > 0) & (f < n_f - 1))(lambda: step(False, False))
        pl.when(f == n_f - 1)(lambda: step(False, True))


def _ffn(x, mod, norm_g, w_gate, w_up, w_down, layer, which):
    bsz, s, d = x.shape
    f = w_gate.shape[-1]
    tm = _tile(s, FFN_ROW_TILE)
    tf = _tile(f, FFN_COL_TILE)
    return pl.pallas_call(
        functools.partial(_ffn_kernel, sub=2 * which, row_groups=2 if tm % 256 == 0 else 1,
                          n_f=f // tf),
        grid=(bsz, s // tm, f // tf),
        in_specs=[pl.BlockSpec((1, tm, d), lambda b, i, j: (b, i, 0)),
                  pl.BlockSpec((1, 1, N_MOD, d), lambda b, i, j: (layer, b, 0, 0)),
                  pl.BlockSpec((1, 3, d), lambda b, i, j: (layer, 0, 0)),
                  pl.BlockSpec((1, 1, d, tf), lambda b, i, j: (layer, which, 0, j)),
                  pl.BlockSpec((1, 1, d, tf), lambda b, i, j: (layer, which, 0, j)),
                  pl.BlockSpec((1, 1, tf, d), lambda b, i, j: (layer, which, j, 0))],
        out_specs=pl.BlockSpec((1, tm, d), lambda b, i, j: (b, i, 0)),
        out_shape=jax.ShapeDtypeStruct((bsz, s, d), F32),
        scratch_shapes=[pltpu.VMEM((tm, d), BF16)],
        compiler_params=_params("parallel", "parallel", "arbitrary"),
        name="ffn",
    )(x, mod, norm_g, w_gate, w_up, w_down)


def _qk_norm_store(acc, gain, group_mean, o_ref, post_scale):
    width = group_mean.shape[0]
    if post_scale != 1.0:
        gain = gain * post_scale
    for j in range(acc.shape[1] // width):
        sl = slice(j * width, (j + 1) * width)
        y = acc[:, sl]
        y2 = y * y
        y2_hi = y2.astype(BF16)
        y2_lo = (y2 - y2_hi.astype(F32)).astype(BF16)
        ms = (jnp.dot(y2_hi, group_mean, preferred_element_type=F32)
              + jnp.dot(y2_lo, group_mean, preferred_element_type=F32))
        yn = (y * lax.rsqrt(ms + RMS_EPS)) * gain[:, sl]
        o_ref[0, :, sl] = yn.astype(o_ref.dtype)


def _proj_in_kernel(x_ref, mod_ref, g_ref, w_ref, *rest, qk_norm, q_scale, tiles_per_section):
    if qk_norm:
        gains_ref, gmean_ref, o_ref, h_ref = rest
    else:
        o_ref, h_ref = rest
    n = pl.program_id(2)

    def column_tile(first):
        if first:
            _norm_modulate_into(h_ref, x_ref, g_ref[0, 1:2, :],
                                mod_ref[0, 0, 3:4, :], mod_ref[0, 0, 4:5, :])
        acc = jnp.dot(h_ref[...], w_ref[0], preferred_element_type=F32)
        if not qk_norm:
            o_ref[0] = acc.astype(o_ref.dtype)
        elif first:
            _qk_norm_store(acc, gains_ref[0:1, :], gmean_ref[...], o_ref, q_scale)
        else:
            section = n // tiles_per_section

            @pl.when(section == 0)
            def _():
                _qk_norm_store(acc, gains_ref[0:1, :], gmean_ref[...], o_ref, q_scale)

            @pl.when(section == 1)
            def _():
                _qk_norm_store(acc, gains_ref[1:2, :], gmean_ref[...], o_ref, 1.0)

            @pl.when(section == 2)
            def _():
                o_ref[0] = acc.astype(o_ref.dtype)

    pl.when(n == 0)(lambda: column_tile(True))
    pl.when(n > 0)(lambda: column_tile(False))


def _proj_in(x, mod, norm_g, w, layer, slot, out_dtype, gains=None, q_scale=1.0):
    bsz, s, d = x.shape
    n = w.shape[-1]
    tm = _tile(s, 1024)
    tn = _tile(d, 1024)
    qk_norm = gains is not None
    in_specs = [pl.BlockSpec((1, tm, d), lambda b, i, j: (b, i, 0)),
                pl.BlockSpec((1, 1, N_MOD, d), lambda b, i, j: (layer, b, 0, 0)),
                pl.BlockSpec((1, 3, d), lambda b, i, j: (layer, 0, 0)),
                pl.BlockSpec((1, d, tn), lambda b, i, j: (slot, 0, j))]
    args = [x, mod, norm_g, w]
    if qk_norm:
        group = gains.shape[1]
        width = _tile(tn, 2 * LANES)
        blk = jnp.arange(width) // group
        gmean = jnp.where(blk[:, None] == blk[None, :], 1.0 / group, 0.0).astype(BF16)
        in_specs += [pl.BlockSpec((2, tn), lambda b, i, j: (0, 0)),
                     pl.BlockSpec((width, width), lambda b, i, j: (0, 0))]
        args += [jnp.tile(gains, (1, tn // group)), gmean]
    return pl.pallas_call(
        functools.partial(_proj_in_kernel, qk_norm=qk_norm, q_scale=q_scale,
                          tiles_per_section=d // tn),
        grid=(bsz, s // tm, n // tn),
        in_specs=in_specs,
        out_specs=pl.BlockSpec((1, tm, tn), lambda b, i, j: (b, i, j)),
        out_shape=jax.ShapeDtypeStruct((bsz, s, n), out_dtype),
        scratch_shapes=[pltpu.VMEM((tm, d), BF16)],
        compiler_params=_params("parallel", "parallel", "arbitrary"),
        name="proj_in",
    )(*args)


def _proj_out_kernel(y_ref, w_ref, x_ref, mod_ref, o_ref):
    o_ref[0] = x_ref[0] + mod_ref[0, 0, 5:6, :] * jnp.dot(
        y_ref[0], w_ref[0], preferred_element_type=F32)


def _proj_out(y, w, x, mod, layer, slot):
    bsz, s, d = x.shape
    k = y.shape[2]
    tm = _tile(s, 512)
    return pl.pallas_call(
        _proj_out_kernel,
        grid=(bsz, s // tm),
        in_specs=[pl.BlockSpec((1, tm, k), lambda b, i: (b, i, 0)),
                  pl.BlockSpec((1, k, d), lambda b, i: (slot, 0, 0)),
                  pl.BlockSpec((1, tm, d), lambda b, i: (b, i, 0)),
                  pl.BlockSpec((1, 1, N_MOD, d), lambda b, i: (layer, b, 0, 0))],
        out_specs=pl.BlockSpec((1, tm, d), lambda b, i: (b, i, 0)),
        out_shape=jax.ShapeDtypeStruct((bsz, s, d), F32),
        compiler_params=_params("parallel", "parallel"),
        name="proj_out",
    )(y, w, x, mod)


ACC_ROWS = LANES + BF16_SUBLANES
N_BIAS_COLS = 6
ATTN_TILE = 512
CARRY_SLOT = 2


def _attn_kernel(skip_ref, q_ref, k_ref, v_ref, slope_ref, qbias_ref, lam_ref, subg_ref,
                 dist_ref, mask_ref, o_ref, kaug_ref, vt_ref, own_ref, s_ref, mt_ref, m_ref,
                 acc_ref, *, tq, tk, lambda_init):
    h = pl.program_id(1)
    n_kv = vt_ref.shape[0]
    n_q = q_ref.shape[1] // tq

    own_ref[...] = slope_ref[0] * dist_ref[...] + mask_ref[...]
    row16 = lax.broadcasted_iota(jnp.int32, (BF16_SUBLANES, tk), 0)
    ones_rows = jnp.where(row16 == 0, 1.0, 0.0).astype(BF16)
    lane = lax.broadcasted_iota(jnp.int32, (tk, LANES), 1)
    row = lax.broadcasted_iota(jnp.int32, (tk, LANES), 0)
    pos = jnp.where(lane < N_BIAS_COLS // 2, row >> 1,
                    jnp.where(lane < N_BIAS_COLS, row & 1, 0)).astype(F32).astype(BF16)
    for c in range(n_kv):
        blk = v_ref[0, c * tk:(c + 1) * tk, :].astype(F32)
        vt_ref[c, 0:LANES, :] = blk.T.astype(BF16)
        vt_ref[c, LANES:ACC_ROWS, :] = ones_rows
        kaug_ref[c, :, 0:LANES] = k_ref[0, c * tk:(c + 1) * tk, :]
        kaug_ref[c, :, LANES:2 * LANES] = pos

    lo = lax.broadcasted_iota(jnp.int32, (1, LANES), 1) < LANES // 2
    q_bias = jnp.broadcast_to(qbias_ref[0], (tq, LANES)).astype(BF16)
    slope2 = slope_ref[0]

    def augmented_queries(qt):
        q = q_ref[0, pl.ds(pl.multiple_of(qt * tq, tq), tq), :]
        zero = jnp.zeros_like(q)
        return (jnp.concatenate([jnp.where(lo, q, zero), q_bias], axis=1),
                jnp.concatenate([jnp.where(lo, zero, q), q_bias], axis=1))

    def first_tile(qt):
        return jnp.maximum(qt * tq - skip_ref[h] + 1, 0) // tk

    def stage_a(j, slot, q_pair):
        kt = kaug_ref[j]
        for mp in range(2):
            s = lax.dot_general(kt, q_pair[mp], NT_DIMS, preferred_element_type=F32)
            s_ref[slot, mp] = s
            mt_ref[slot, mp] = jnp.max(s, axis=0, keepdims=True)

    def accumulate(mp, z, tile_max, shift, vt):
        m_old = m_ref[mp]
        m_new = jnp.maximum(m_old, tile_max + shift)
        alpha = jnp.exp2(m_old - m_new)
        p = jnp.exp2(z - (m_new - shift)).astype(BF16)
        acc_ref[mp] = alpha * acc_ref[mp] + jnp.dot(vt, p, preferred_element_type=F32)
        m_ref[mp] = m_new

    stage_a(0, CARRY_SLOT, augmented_queries(0))

    def query_tile(qi, carry):
        q_aug = augmented_queries(qi)
        acc_ref[...] = jnp.zeros_like(acc_ref)
        m_ref[...] = jnp.full(m_ref.shape, NEG_BIG, F32)

        def stage_b(j, slot):
            shift = slope2 * (j * tk - qi * tq).astype(F32)
            vt = vt_ref[j]
            for mp in range(2):
                accumulate(mp, s_ref[slot, mp], mt_ref[slot, mp], shift, vt)

        def stage_b_own(slot):
            vt = vt_ref[qi]
            for mp in range(2):
                m_old = m_ref[mp]
                m_cols, p_cols = [], []
                for cb in range(tq // LANES):
                    cols = slice(cb * LANES, (cb + 1) * LANES)
                    rows = min((cb + 1) * LANES, tk)
                    z = s_ref[slot, mp, 0:rows, cols] + own_ref[0:rows, cols]
                    m_new = jnp.maximum(m_old[:, cols], jnp.max(z, axis=0, keepdims=True))
                    p = jnp.exp2(z - m_new).astype(BF16)
                    if rows < tk:
                        p = jnp.concatenate([p, jnp.zeros((tk - rows, LANES), BF16)], axis=0)
                    m_cols.append(m_new)
                    p_cols.append(p)
                m_new = jnp.concatenate(m_cols, axis=1)
                alpha = jnp.exp2(m_old - m_new)
                acc_ref[mp] = alpha * acc_ref[mp] + jnp.dot(
                    vt, jnp.concatenate(p_cols, axis=1), preferred_element_type=F32)
                m_ref[mp] = m_new

        def finish():
            qn = jnp.minimum(qi + 1, n_q - 1)
            stage_a(first_tile(qn), CARRY_SLOT, augmented_queries(qn))
            lam_v = lam_ref[...]
            lam = (jnp.exp(jnp.sum(lam_v[0:1] * lam_v[1:2], axis=-1, keepdims=True))
                   - jnp.exp(jnp.sum(lam_v[2:3] * lam_v[3:4], axis=-1, keepdims=True))
                   + lambda_init)
            o = (acc_ref[0, 0:LANES, :] / acc_ref[0, LANES:LANES + 1, :]
                 - lam * (acc_ref[1, 0:LANES, :] / acc_ref[1, LANES:LANES + 1, :]))
            ms = jnp.mean(o * o, axis=0, keepdims=True)
            on = (o * lax.rsqrt(ms + RMS_EPS)).T
            o_ref[0, pl.ds(pl.multiple_of(qi * tq, tq), tq), :] = (
                (on * subg_ref[...]) * (1.0 - lambda_init)).astype(o_ref.dtype)

        j0 = first_tile(qi)
        count = qi - j0
        rest = jnp.maximum(count - 1, 0)

        @pl.when(count == 0)
        def _():
            stage_b_own(CARRY_SLOT)
            finish()

        @pl.when(count > 0)
        def _():
            stage_a(j0 + 1, 0, q_aug)
            stage_b(j0, CARRY_SLOT)

        def pair_at(j):
            stage_a(j + 1, 1, q_aug)
            stage_b(j, 0)
            stage_a(j + 2, 0, q_aug)
            stage_b(j + 1, 1)

        def quad(t, c):
            j = j0 + 1 + 4 * t
            pair_at(j)
            pair_at(j + 2)
            return c

        def pair(t, c):
            pair_at(j0 + 1 + 4 * (rest // 4) + 2 * t)
            return c

        lax.fori_loop(0, rest // 4, quad, 0)
        lax.fori_loop(0, (rest % 4) // 2, pair, 0)

        @pl.when((count > 0) & (rest % 2 == 0))
        def _():
            stage_b_own(0)
            finish()

        @pl.when((count > 0) & (rest % 2 == 1))
        def _():
            stage_a(qi, 1, q_aug)
            stage_b(qi - 1, 0)
            stage_b_own(1)
            finish()

        return carry

    lax.fori_loop(0, n_q, query_tile, 0)


def _attention(qkv, slopes2, logit_bound, lam_vecs, subln_g, lambda_init, n_heads):
    bsz, s, d3 = qkv.shape
    d = d3 // 3
    assert d // n_heads == LANES
    tq = _tile(s, ATTN_TILE)
    tk = tq
    assert tk <= 2 * 256
    assert LANES % CHUNK == 0 and tq % LANES == 0

    r = jnp.arange(tk)[:, None]
    c = jnp.arange(tq)[None, :]
    dist = jnp.minimum(0, 2 * (c - r)).astype(F32)
    mask = jnp.where(r // CHUNK <= c // CHUNK, 0.0, NEG_BIG).astype(F32)

    slope_rows = jnp.broadcast_to(slopes2[:, None, None], (n_heads, 1, tq)).astype(F32)
    hi, mid, lo = (p.astype(F32) for p in _split3(slopes2))
    pieces = jnp.stack([2 * hi, 2 * mid, 2 * lo, hi, mid, lo], axis=-1)
    qbias = jnp.zeros((n_heads, 1, LANES), F32).at[:, 0, :N_BIAS_COLS].set(pieces)
    skip = jnp.ceil((F32_MIN_EXP + F32_MANT_BITS + 2.0 * logit_bound) / slopes2)
    skip = jnp.clip(skip, 1, 2 ** 30).astype(jnp.int32)

    grid_spec = pltpu.PrefetchScalarGridSpec(
        num_scalar_prefetch=1,
        grid=(bsz, n_heads),
        in_specs=[pl.BlockSpec((1, s, LANES), lambda b, h, sk: (b, 0, h)),
                  pl.BlockSpec((1, s, LANES), lambda b, h, sk: (b, 0, n_heads + h)),
                  pl.BlockSpec((1, s, LANES), lambda b, h, sk: (b, 0, 2 * n_heads + h)),
                  pl.BlockSpec((1, 1, tq), lambda b, h, sk: (h, 0, 0)),
                  pl.BlockSpec((1, 1, LANES), lambda b, h, sk: (h, 0, 0)),
                  pl.BlockSpec(lam_vecs.shape, lambda b, h, sk: (0, 0)),
                  pl.BlockSpec((1, LANES), lambda b, h, sk: (0, 0)),
                  pl.BlockSpec((tk, tq), lambda b, h, sk: (0, 0)),
                  pl.BlockSpec((tk, tq), lambda b, h, sk: (0, 0))],
        out_specs=pl.BlockSpec((1, s, LANES), lambda b, h, sk: (b, 0, h)),
        scratch_shapes=[pltpu.VMEM((s // tk, tk, 2 * LANES), BF16),
                        pltpu.VMEM((s // tk, ACC_ROWS, tk), BF16),
                        pltpu.VMEM((tk, tq), F32),
                        pltpu.VMEM((3, 2, tk, tq), F32),
                        pltpu.VMEM((3, 2, 1, tq), F32),
                        pltpu.VMEM((2, 1, tq), F32),
                        pltpu.VMEM((2, ACC_ROWS, tq), F32)])
    return pl.pallas_call(
        functools.partial(_attn_kernel, tq=tq, tk=tk, lambda_init=lambda_init),
        grid_spec=grid_spec,
        out_shape=jax.ShapeDtypeStruct((bsz, s, d), BF16),
        compiler_params=_params("parallel", "parallel"),
        name="diff_attn",
    )(skip, qkv, qkv, qkv, slope_rows, qbias, lam_vecs, subln_g.reshape(1, LANES), dist, mask)


CONV_HALO = BF16_SUBLANES


def _conv_in_kernel(x_ref, xh_ref, mod_ref, g_ref, wb_ref, wc_ref, wu_ref, cw_ref, o_ref,
                    h_ref, ext_ref):
    i = pl.program_id(1)
    n = pl.program_id(2)
    tm = x_ref.shape[1]
    halo = CONV_HALO

    def column_tile(first):
        if first:
            g, shift, scale = g_ref[0, 1:2, :], mod_ref[0, 0, 3:4, :], mod_ref[0, 0, 4:5, :]
            _norm_modulate_into(h_ref.at[0:halo], xh_ref, g, shift, scale)
            _norm_modulate_into(h_ref.at[halo:halo + tm], x_ref, g, shift, scale)
        h_all = h_ref[...]
        c = jnp.dot(h_all, wc_ref[0], preferred_element_type=F32)
        u = jnp.dot(h_all, wu_ref[0], preferred_element_type=F32)
        b = jnp.dot(h_ref[halo:halo + tm, :], wb_ref[0], preferred_element_type=F32)
        v = c * u
        ext_ref[0:halo, :] = jnp.where(i == 0, 0.0, v[0:halo])
        ext_ref[halo:, :] = v[halo:]
        v1 = ext_ref[halo - 1:halo - 1 + tm, :]
        v2 = ext_ref[halo - 2:halo - 2 + tm, :]
        w = cw_ref[0]
        y = w[0:1] * v2 + w[1:2] * v1 + w[2:3] * v[halo:]
        o_ref[0] = (b * y).astype(o_ref.dtype)

    pl.when(n == 0)(lambda: column_tile(True))
    pl.when(n > 0)(lambda: column_tile(False))


def _conv_in(x, mod, norm_g, w_in, conv_w, layer, slot):
    bsz, s, d = x.shape
    assert CONV_WIDTH - 1 <= CONV_HALO
    tm = _tile(s, 1024)
    tn = _tile(d, 512)
    nc = d // tn
    hb = tm // CONV_HALO
    return pl.pallas_call(
        _conv_in_kernel,
        grid=(bsz, s // tm, nc),
        in_specs=[pl.BlockSpec((1, tm, d), lambda b, i, j: (b, i, 0)),
                  pl.BlockSpec((1, CONV_HALO, d),
                               lambda b, i, j: (b, jnp.maximum(i * hb - 1, 0), 0)),
                  pl.BlockSpec((1, 1, N_MOD, d), lambda b, i, j: (layer, b, 0, 0)),
                  pl.BlockSpec((1, 3, d), lambda b, i, j: (layer, 0, 0)),
                  pl.BlockSpec((1, d, tn), lambda b, i, j: (slot, 0, j)),
                  pl.BlockSpec((1, d, tn), lambda b, i, j: (slot, 0, nc + j)),
                  pl.BlockSpec((1, d, tn), lambda b, i, j: (slot, 0, 2 * nc + j)),
                  pl.BlockSpec((1, CONV_WIDTH, tn), lambda b, i, j: (slot, 0, j))],
        out_specs=pl.BlockSpec((1, tm, tn), lambda b, i, j: (b, i, j)),
        out_shape=jax.ShapeDtypeStruct((bsz, s, d), BF16),
        scratch_shapes=[pltpu.VMEM((CONV_HALO + tm, d), BF16),
                        pltpu.VMEM((CONV_HALO + tm, tn), F32)],
        compiler_params=_params("parallel", "parallel", "arbitrary"),
        name="conv_in",
    )(x, x, mod, norm_g, w_in, w_in, w_in, conv_w)


HGRN_CHUNKS_PER_STEP = 8
HGRN_TIME_TILE = 8192


def _hgrn_kernel(q_ref, fl_ref, i_ref, g_ref, lbl_ref, ong_ref, o_ref,
                 state_ref, b_ref, *, layer, ts):
    t = pl.program_id(2)
    c64 = CHUNK
    sb = SUBLANES

    @pl.when(t == 0)
    def _():
        state_ref[...] = jnp.zeros_like(state_ref)

    lbl = lbl_ref[...]
    e = jnp.exp(lbl - jnp.max(lbl, axis=0, keepdims=True))
    sm = e / jnp.sum(e, axis=0, keepdims=True)
    lb = jnp.zeros((1, LANES), F32)
    for r in range(1, layer + 1):
        lb = lb + sm[r:r + 1]

    row = lax.broadcasted_iota(jnp.int32, (c64, c64), 0)
    col = lax.broadcasted_iota(jnp.int32, (c64, c64), 1)
    tri = jnp.where(row >= col, 1.0, 0.0).astype(BF16)
    ones = jnp.ones((LANES, LANES), BF16)
    sub_row = lax.broadcasted_iota(jnp.int32, (sb, LANES), 0)
    col8 = lax.broadcasted_iota(jnp.int32, (sb, c64), 1)
    groups = (2 * sb, 4 * sb, 8 * sb)
    level_masks = [(row // grp == col // grp) & (row % grp >= grp // 2) & (col % grp < grp // 2)
                   for grp in groups]


    def gates_and_cumsum(r0, u):
        fl = fl_ref[0, pl.ds(r0, c64), :]
        f = lb + (1.0 - lb) * jax.nn.sigmoid(fl)
        lf = jnp.log2(f)
        b = jnp.zeros((c64, LANES), F32)
        for piece in _split3(lf):
            b = b + jnp.dot(tri, piece, preferred_element_type=F32)
        b_ref[u] = b
        return dict(q=q_ref[0, pl.ds(r0, c64), :], kk=1.0 - f, b=b,
                    i16=i_ref[0, pl.ds(r0, c64), :].astype(BF16))

    def diagonal_sums(ch, u):
        w_rows = []
        for blk in range(c64 // sb):
            b_blk = ch["b"][blk * sb:(blk + 1) * sb]
            q_blk = ch["q"][blk * sb:(blk + 1) * sb]
            for s_ in range(sb):
                src = blk * sb + s_
                dec = jnp.exp2(b_blk - b_ref[u, src:src + 1, :])
                w_rows.append(jnp.where(sub_row >= s_,
                                        q_blk * dec * ch["kk"][src:src + 1, :], 0.0))
        w_all = jnp.concatenate(w_rows, axis=0).astype(BF16)
        ch["sums"] = jnp.dot(w_all, ones, preferred_element_type=F32)

    def level_scores(ch, u):
        q, kk, b = ch["q"], ch["kk"], ch["b"]
        out = []
        for grp in groups:
            parts = []
            for gidx in range(c64 // grp):
                edge = gidx * grp + grp // 2 - 1
                parts.append(jnp.broadcast_to(b_ref[u, edge:edge + 1, :], (grp, LANES)))
            ref = parts[0] if len(parts) == 1 else jnp.concatenate(parts, axis=0)
            qd = (q * jnp.exp2(jnp.minimum(b - ref, 0.0))).astype(BF16)
            kd = (kk * jnp.exp2(jnp.minimum(ref - b, 0.0))).astype(BF16)
            out.append(lax.dot_general(qd, kd, NT_DIMS, preferred_element_type=F32))
        ch["levels"] = out
        b_last = b_ref[u, c64 - 1:c64, :]
        kd_end = (kk * jnp.exp2(b_last - b)).astype(BF16)
        ch["upd"] = lax.dot_general(ch["i16"], kd_end, TN_DIMS,
                                    preferred_element_type=F32)
        ch["decay"] = jnp.exp2(b_last)
        ch["q_in"] = (q * jnp.exp2(b)).astype(BF16)

    def intra_chunk(ch):
        score_rows = []
        for blk in range(c64 // sb):
            acc = jnp.zeros((sb, c64), F32)
            for s_ in range(sb):
                idx = blk * sb + s_
                acc = jnp.where(col8 == idx, ch["sums"][idx * sb:(idx + 1) * sb, :c64], acc)
            score_rows.append(acc)
        scores = jnp.concatenate(score_rows, axis=0)
        for lvl in range(len(groups)):
            scores = scores + jnp.where(level_masks[lvl], ch["levels"][lvl], 0.0)
        ch["intra"] = jnp.dot(scores.astype(BF16), ch["i16"], preferred_element_type=F32)

    per_step = b_ref.shape[0]

    def step(c, carry):
        base = c * (per_step * c64)
        starts = [pl.multiple_of(base + u * c64, c64) for u in range(per_step)]
        chunks = [gates_and_cumsum(r0, u) for u, r0 in enumerate(starts)]
        for u, ch in enumerate(chunks):
            diagonal_sums(ch, u)
        for u, ch in enumerate(chunks):
            level_scores(ch, u)
        for ch in chunks:
            intra_chunk(ch)
        st = state_ref[...]
        for r0, ch in zip(starts, chunks):
            inter = lax.dot_general(ch["q_in"], st.astype(BF16), NT_DIMS,
                                    preferred_element_type=F32)
            st = st * ch["decay"] + ch["upd"]
            o = inter + ch["intra"]
            ms = jnp.mean(o * o, axis=-1, keepdims=True)
            on = (o * lax.rsqrt(ms + RMS_EPS)) * ong_ref[...]
            gv = g_ref[0, pl.ds(r0, c64), :]
            o_ref[0, pl.ds(r0, c64), :] = (on * (gv * jax.nn.sigmoid(gv))).astype(o_ref.dtype)
        state_ref[...] = st
        return carry

    lax.fori_loop(0, ts // (per_step * c64), step, 0)


def _hgrn_core(proj, lb_logits, o_norm_g, layer, n_heads):
    bsz, s, d4 = proj.shape
    d = d4 // 4
    assert d // n_heads == LANES
    ts = _tile(s, HGRN_TIME_TILE)
    per_step = math.gcd(HGRN_CHUNKS_PER_STEP, ts // CHUNK)
    depth = lb_logits.shape[0]
    return pl.pallas_call(
        functools.partial(_hgrn_kernel, layer=layer, ts=ts),
        grid=(bsz, n_heads, s // ts),
        in_specs=[pl.BlockSpec((1, ts, LANES), lambda b, h, t: (b, t, h)),
                  pl.BlockSpec((1, ts, LANES), lambda b, h, t: (b, t, n_heads + h)),
                  pl.BlockSpec((1, ts, LANES), lambda b, h, t: (b, t, 2 * n_heads + h)),
                  pl.BlockSpec((1, ts, LANES), lambda b, h, t: (b, t, 3 * n_heads + h)),
                  pl.BlockSpec((depth, LANES), lambda b, h, t: (0, h)),
                  pl.BlockSpec((1, LANES), lambda b, h, t: (0, 0))],
        out_specs=pl.BlockSpec((1, ts, LANES), lambda b, h, t: (b, t, h)),
        out_shape=jax.ShapeDtypeStruct((bsz, s, d), BF16),
        scratch_shapes=[pltpu.VMEM((LANES, LANES), F32),
                        pltpu.VMEM((per_step, CHUNK, LANES), F32)],
        compiler_params=_params("parallel", "parallel", "arbitrary"),
        name="hgrn2",
    )(proj, proj, proj, proj, lb_logits, o_norm_g.reshape(1, LANES))


def kernel(x, c, ada_w, ada_b, norm_g, ffn_w_gate, ffn_w_up, ffn_w_down,
           attn_w_in, attn_w_out, attn_q_gain, attn_k_gain, attn_lambda, attn_subln_g,
           conv_w_in, conv_w, conv_w_out,
           hgrn_w_in, hgrn_w_out, hgrn_o_norm_g, hgrn_lb_logits):
    depth = ada_w.shape[0]
    d = x.shape[-1]
    qk_dim = attn_q_gain.shape[-1]
    attn_heads = d // (2 * qk_dim)
    hgrn_heads = d // hgrn_o_norm_g.shape[-1]
    ffn_w_gate, ffn_w_up, ffn_w_down = (w.astype(BF16) for w in (ffn_w_gate, ffn_w_up, ffn_w_down))
    attn_w_in, attn_w_out = attn_w_in.astype(BF16), attn_w_out.astype(BF16)
    conv_w_in, conv_w_out = conv_w_in.astype(BF16), conv_w_out.astype(BF16)
    hgrn_w_in, hgrn_w_out = hgrn_w_in.astype(BF16), hgrn_w_out.astype(BF16)

    mod = _ada_mod(c, ada_w, ada_b)
    slopes2 = jnp.asarray(
        [2.0 ** (-8.0 * (h + 1) / attn_heads) * LOG2E for h in range(attn_heads)], F32)
    q_scale = qk_dim ** -0.5 * LOG2E

    for layer in range(depth):
        x = _ffn(x, mod, norm_g, ffn_w_gate, ffn_w_up, ffn_w_down, layer, 0)
        kind, slot = layer % N_MIXERS, layer // N_MIXERS
        if kind == 0:
            lambda_init = 0.8 - 0.6 * math.exp(-0.3 * layer)
            gains = jnp.stack([attn_q_gain[slot], attn_k_gain[slot]])
            qkv = _proj_in(x, mod, norm_g, attn_w_in, layer, slot, BF16,
                           gains=gains, q_scale=q_scale)
            logit_bound = (1.02 * qk_dim * q_scale * jnp.max(jnp.abs(attn_q_gain[slot]))
                           * jnp.max(jnp.abs(attn_k_gain[slot])))
            y = _attention(qkv, slopes2, logit_bound, attn_lambda[slot], attn_subln_g[slot],
                           lambda_init, attn_heads)
            x = _proj_out(y, attn_w_out, x, mod, layer, slot)
        elif kind == 1:
            y = _conv_in(x, mod, norm_g, conv_w_in, conv_w, layer, slot)
            x = _proj_out(y, conv_w_out, x, mod, layer, slot)
        else:
            proj = _proj_in(x, mod, norm_g, hgrn_w_in, layer, slot, F32)
            y = _hgrn_core(proj, hgrn_lb_logits, hgrn_o_norm_g[slot], layer, hgrn_heads)
            x = _proj_out(y, hgrn_w_out, x, mod, layer, slot)
        x = _ffn(x, mod, norm_g, ffn_w_gate, ffn_w_up, ffn_w_down, layer, 1)
    return x
```

```python
import functools
import math

import jax
import jax.numpy as jnp
from jax import lax
from jax.experimental import pallas as pl
from jax.experimental.pallas import tpu as pltpu

F32 = jnp.float32
BF16 = jnp.bfloat16

RMS_EPS = 1e-6
CHUNK = 64
N_MIXERS = 3
N_MOD = 9
CONV_WIDTH = 3
LOG2E = 1.4426950408889634
NEG_BIG = -1e30
F32_MIN_EXP = 127
F32_MANT_BITS = 24

LANES = 128
SUBLANES = 8
BF16_SUBLANES = 16
NORM_ROW_BLOCK = 16
V7X_VMEM_LIMIT_BYTES = 60 * 1024 * 1024

NT_DIMS = (((1,), (1,)), ((), ()))
TN_DIMS = (((0,), (0,)), ((), ()))


def _tile(n, preferred):
    for t in (preferred, 2048, 1024, 512, 256, 128, 64, 32, 16, 8):
        if t <= preferred and n % t == 0:
            return t
    return n


def _params(*sem):
    return pltpu.CompilerParams(dimension_semantics=sem,
                                vmem_limit_bytes=V7X_VMEM_LIMIT_BYTES)


def _norm_modulate_into(h_ref, x_ref, g, shift, scale):
    gain = g * (1.0 + scale)
    rows = h_ref.shape[0]
    blk = min(rows, NORM_ROW_BLOCK)
    for r in range(0, rows, blk):
        x = x_ref[0, r:r + blk, :]
        ms = jnp.mean(x * x, axis=-1, keepdims=True)
        h_ref[r:r + blk, :] = ((x * lax.rsqrt(ms + RMS_EPS)) * gain + shift).astype(BF16)


def _split3(x):
    hi = x.astype(BF16)
    r1 = x - hi.astype(F32)
    mid = r1.astype(BF16)
    lo = (r1 - mid.astype(F32)).astype(BF16)
    return hi, mid, lo


def _ada_kernel(c_ref, w_ref, b_ref, o_ref):
    c = c_ref[...]
    cond = (c * jax.nn.sigmoid(c)).astype(BF16)
    o_ref[0] = jnp.dot(cond, w_ref[0].astype(BF16),
                       preferred_element_type=F32) + b_ref[0]


def _ada_mod(c, ada_w, ada_b):
    depth, d, n = ada_w.shape
    bsz = c.shape[0]
    rows = -(-bsz // SUBLANES) * SUBLANES
    c_pad = jnp.zeros((rows, d), F32).at[:bsz].set(c)
    tn = _tile(n, 1024)
    out = pl.pallas_call(
        _ada_kernel,
        grid=(depth, n // tn),
        in_specs=[pl.BlockSpec((rows, d), lambda l, j: (0, 0)),
                  pl.BlockSpec((1, d, tn), lambda l, j: (l, 0, j)),
                  pl.BlockSpec((1, 1, tn), lambda l, j: (l, 0, j))],
        out_specs=pl.BlockSpec((1, rows, tn), lambda l, j: (l, 0, j)),
        out_shape=jax.ShapeDtypeStruct((depth, rows, n), F32),
        compiler_params=_params("parallel", "parallel"),
        name="ada_mod",
    )(c_pad, ada_w, ada_b.reshape(depth, 1, n))
    return out[:, :bsz].reshape(depth, bsz, N_MOD, d)


FFN_ROW_TILE = 1024
FFN_COL_TILE = 512


def _ffn_kernel(x_ref, mod_ref, g_ref, wg_ref, wu_ref, wd_ref, o_ref, h_ref, *, sub, row_groups,
                n_f):
    f = pl.program_id(2)
    r = 3 * sub

    @pl.when(f == 0)
    def _():
        _norm_modulate_into(h_ref, x_ref, g_ref[0, sub:sub + 1, :],
                            mod_ref[0, 0, r:r + 1, :], mod_ref[0, 0, r + 1:r + 2, :])

    rows = h_ref.shape[0] // row_groups
    slices = [slice(grp * rows, (grp + 1) * rows) for grp in range(row_groups)]

    def step(first, last):
        gated = []
        for sl in slices:
            h = h_ref[sl, :]
            a = jnp.dot(h, wg_ref[0, 0], preferred_element_type=F32)
            b = jnp.dot(h, wu_ref[0, 0], preferred_element_type=F32)
            gated.append((a * jax.nn.sigmoid(a) * b).astype(BF16))
        for sl, p in zip(slices, gated):
            total = jnp.dot(p, wd_ref[0, 0], preferred_element_type=F32)
            if not first:
                total = o_ref[0, sl, :] + total
            if last:
                total = x_ref[0, sl, :] + (0.5 * mod_ref[0, 0, r + 2:r + 3, :]) * total
            o_ref[0, sl, :] = total

    if n_f == 1:
        step(True, True)
    else:
        pl.when(f == 0)(lambda: step(True, False))
        if n_f > 2:
            pl.when((f > 0) & (f < n_f - 1))(lambda: step(False, False))
        pl.when(f == n_f - 1)(lambda: step(False, True))


def _ffn(x, mod, norm_g, w_gate, w_up, w_down, layer, which):
    bsz, s, d = x.shape
    f = w_gate.shape[-1]
    tm = _tile(s, FFN_ROW_TILE)
    tf = _tile(f, FFN_COL_TILE)
    return pl.pallas_call(
        functools.partial(_ffn_kernel, sub=2 * which, row_groups=4 if tm % 1024 == 0 else 1,
                          n_f=f // tf),
        grid=(bsz, s // tm, f // tf),
        in_specs=[pl.BlockSpec((1, tm, d), lambda b, i, j: (b, i, 0)),
                  pl.BlockSpec((1, 1, N_MOD, d), lambda b, i, j: (layer, b, 0, 0)),
                  pl.BlockSpec((1, 3, d), lambda b, i, j: (layer, 0, 0)),
                  pl.BlockSpec((1, 1, d, tf), lambda b, i, j: (layer, which, 0, j)),
                  pl.BlockSpec((1, 1, d, tf), lambda b, i, j: (layer, which, 0, j)),
                  pl.BlockSpec((1, 1, tf, d), lambda b, i, j: (layer, which, j, 0))],
        out_specs=pl.BlockSpec((1, tm, d), lambda b, i, j: (b, i, 0)),
        out_shape=jax.ShapeDtypeStruct((bsz, s, d), F32),
        scratch_shapes=[pltpu.VMEM((tm, d), BF16)],
        compiler_params=_params("parallel", "parallel", "arbitrary"),
        name="ffn",
    )(x, mod, norm_g, w_gate, w_up, w_down)


def _qk_norm_store(acc, gain, group_mean, o_ref, post_scale):
    width = group_mean.shape[0]
    if post_scale != 1.0:
        gain = gain * post_scale
    for j in range(acc.shape[1] // width):
        sl = slice(j * width, (j + 1) * width)
        y = acc[:, sl]
        y2 = y * y
        y2_hi = y2.astype(BF16)
        y2_lo = (y2 - y2_hi.astype(F32)).astype(BF16)
        ms = (jnp.dot(y2_hi, group_mean, preferred_element_type=F32)
              + jnp.dot(y2_lo, group_mean, preferred_element_type=F32))
        yn = (y * lax.rsqrt(ms + RMS_EPS)) * gain[:, sl]
        o_ref[0, :, sl] = yn.astype(o_ref.dtype)


def _proj_in_kernel(x_ref, mod_ref, g_ref, w_ref, *rest, qk_norm, q_scale, tiles_per_section):
    if qk_norm:
        gains_ref, gmean_ref, o_ref, h_ref = rest
    else:
        o_ref, h_ref = rest
    n = pl.program_id(2)

    def column_tile(first):
        if first:
            _norm_modulate_into(h_ref, x_ref, g_ref[0, 1:2, :],
                                mod_ref[0, 0, 3:4, :], mod_ref[0, 0, 4:5, :])
        acc = jnp.dot(h_ref[...], w_ref[0], preferred_element_type=F32)
        if not qk_norm:
            o_ref[0] = acc.astype(o_ref.dtype)
        elif first:
            _qk_norm_store(acc, gains_ref[0:1, :], gmean_ref[...], o_ref, q_scale)
        else:
            section = n // tiles_per_section

            @pl.when(section == 0)
            def _():
                _qk_norm_store(acc, gains_ref[0:1, :], gmean_ref[...], o_ref, q_scale)

            @pl.when(section == 1)
            def _():
                _qk_norm_store(acc, gains_ref[1:2, :], gmean_ref[...], o_ref, 1.0)

            @pl.when(section == 2)
            def _():
                o_ref[0] = acc.astype(o_ref.dtype)

    pl.when(n == 0)(lambda: column_tile(True))
    pl.when(n > 0)(lambda: column_tile(False))


def _proj_in(x, mod, norm_g, w, layer, slot, out_dtype, gains=None, q_scale=1.0):
    bsz, s, d = x.shape
    n = w.shape[-1]
    tm = _tile(s, 1024)
    tn = _tile(d, 1024)
    qk_norm = gains is not None
    in_specs = [pl.BlockSpec((1, tm, d), lambda b, i, j: (b, i, 0)),
                pl.BlockSpec((1, 1, N_MOD, d), lambda b, i, j: (layer, b, 0, 0)),
                pl.BlockSpec((1, 3, d), lambda b, i, j: (layer, 0, 0)),
                pl.BlockSpec((1, d, tn), lambda b, i, j: (slot, 0, j))]
    args = [x, mod, norm_g, w]
    if qk_norm:
        group = gains.shape[1]
        width = _tile(tn, 2 * LANES)
        blk = jnp.arange(width) // group
        gmean = jnp.where(blk[:, None] == blk[None, :], 1.0 / group, 0.0).astype(BF16)
        in_specs += [pl.BlockSpec((2, tn), lambda b, i, j: (0, 0)),
                     pl.BlockSpec((width, width), lambda b, i, j: (0, 0))]
        args += [jnp.tile(gains, (1, tn // group)), gmean]
    return pl.pallas_call(
        functools.partial(_proj_in_kernel, qk_norm=qk_norm, q_scale=q_scale,
                          tiles_per_section=d // tn),
        grid=(bsz, s // tm, n // tn),
        in_specs=in_specs,
        out_specs=pl.BlockSpec((1, tm, tn), lambda b, i, j: (b, i, j)),
        out_shape=jax.ShapeDtypeStruct((bsz, s, n), out_dtype),
        scratch_shapes=[pltpu.VMEM((tm, d), BF16)],
        compiler_params=_params("parallel", "parallel", "arbitrary"),
        name="proj_in",
    )(*args)


def _proj_out_kernel(y_ref, w_ref, x_ref, mod_ref, o_ref):
    o_ref[0] = x_ref[0] + mod_ref[0, 0, 5:6, :] * jnp.dot(
        y_ref[0], w_ref[0], preferred_element_type=F32)


def _proj_out(y, w, x, mod, layer, slot):
    bsz, s, d = x.shape
    k = y.shape[2]
    tm = _tile(s, 512)
    return pl.pallas_call(
        _proj_out_kernel,
        grid=(bsz, s // tm),
        in_specs=[pl.BlockSpec((1, tm, k), lambda b, i: (b, i, 0)),
                  pl.BlockSpec((1, k, d), lambda b, i: (slot, 0, 0)),
                  pl.BlockSpec((1, tm, d), lambda b, i: (b, i, 0)),
                  pl.BlockSpec((1, 1, N_MOD, d), lambda b, i: (layer, b, 0, 0))],
        out_specs=pl.BlockSpec((1, tm, d), lambda b, i: (b, i, 0)),
        out_shape=jax.ShapeDtypeStruct((bsz, s, d), F32),
        compiler_params=_params("parallel", "parallel"),
        name="proj_out",
    )(y, w, x, mod)


ACC_ROWS = LANES + BF16_SUBLANES
N_BIAS_COLS = 6
ATTN_TILE = 512
CARRY_SLOT = 2


def _attn_kernel(skip_ref, q_ref, k_ref, v_ref, slope_ref, qbias_ref, lam_ref, subg_ref,
                 dist_ref, mask_ref, o_ref, kaug_ref, vt_ref, own_ref, s_ref, mt_ref, m_ref,
                 acc_ref, *, tq, tk, lambda_init):
    h = pl.program_id(1)
    n_kv = vt_ref.shape[0]
    n_q = q_ref.shape[1] // tq

    own_ref[...] = slope_ref[0] * dist_ref[...] + mask_ref[...]
    row16 = lax.broadcasted_iota(jnp.int32, (BF16_SUBLANES, tk), 0)
    ones_rows = jnp.where(row16 == 0, 1.0, 0.0).astype(BF16)
    lane = lax.broadcasted_iota(jnp.int32, (tk, LANES), 1)
    row = lax.broadcasted_iota(jnp.int32, (tk, LANES), 0)
    pos = jnp.where(lane < N_BIAS_COLS // 2, row >> 1,
                    jnp.where(lane < N_BIAS_COLS, row & 1, 0)).astype(F32).astype(BF16)
    for c in range(n_kv):
        blk = v_ref[0, c * tk:(c + 1) * tk, :].astype(F32)
        vt_ref[c, 0:LANES, :] = blk.T.astype(BF16)
        vt_ref[c, LANES:ACC_ROWS, :] = ones_rows
        kaug_ref[c, :, 0:LANES] = k_ref[0, c * tk:(c + 1) * tk, :]
        kaug_ref[c, :, LANES:2 * LANES] = pos

    lo = lax.broadcasted_iota(jnp.int32, (1, LANES), 1) < LANES // 2
    q_bias = jnp.broadcast_to(qbias_ref[0], (tq, LANES)).astype(BF16)
    slope2 = slope_ref[0]

    def augmented_queries(qt):
        start = qt * tq
        if not isinstance(start, int):
            start = pl.multiple_of(start, tq)
        q = q_ref[0, pl.ds(start, tq), :]
        zero = jnp.zeros_like(q)
        return (jnp.concatenate([jnp.where(lo, q, zero), q_bias], axis=1),
                jnp.concatenate([jnp.where(lo, zero, q), q_bias], axis=1))

    def first_tile(qt):
        return jnp.maximum(qt * tq - skip_ref[h] + 1, 0) // tk

    def stage_a(j, slot, q_pair):
        kt = kaug_ref[j]
        for mp in range(2):
            s = lax.dot_general(kt, q_pair[mp], NT_DIMS, preferred_element_type=F32)
            s_ref[slot, mp] = s
            mt_ref[slot, mp] = jnp.max(s, axis=0, keepdims=True)

    def accumulate(mp, z, tile_max, shift, vt):
        m_old = m_ref[mp]
        m_new = jnp.maximum(m_old, tile_max + shift)
        alpha = jnp.exp2(m_old - m_new)
        p = jnp.exp2(z - (m_new - shift)).astype(BF16)
        acc_ref[mp] = alpha * acc_ref[mp] + jnp.dot(vt, p, preferred_element_type=F32)
        m_ref[mp] = m_new

    stage_a(0, CARRY_SLOT, augmented_queries(0))

    def query_tile(qi, carry):
        q_aug = augmented_queries(qi)
        acc_ref[...] = jnp.zeros_like(acc_ref)
        m_ref[...] = jnp.full(m_ref.shape, NEG_BIG, F32)

        def stage_b(j, slot):
            shift = slope2 * (j * tk - qi * tq).astype(F32)
            vt = vt_ref[j]
            for mp in range(2):
                accumulate(mp, s_ref[slot, mp], mt_ref[slot, mp], shift, vt)

        def stage_b_own(slot):
            vt = vt_ref[qi]
            for mp in range(2):
                m_old = m_ref[mp]
                m_cols, p_cols = [], []
                for cb in range(tq // LANES):
                    cols = slice(cb * LANES, (cb + 1) * LANES)
                    rows = min((cb + 1) * LANES, tk)
                    z = s_ref[slot, mp, 0:rows, cols] + own_ref[0:rows, cols]
                    m_new = jnp.maximum(m_old[:, cols], jnp.max(z, axis=0, keepdims=True))
                    p = jnp.exp2(z - m_new).astype(BF16)
                    if rows < tk:
                        p = jnp.concatenate([p, jnp.zeros((tk - rows, LANES), BF16)], axis=0)
                    m_cols.append(m_new)
                    p_cols.append(p)
                m_new = jnp.concatenate(m_cols, axis=1)
                alpha = jnp.exp2(m_old - m_new)
                acc_ref[mp] = alpha * acc_ref[mp] + jnp.dot(
                    vt, jnp.concatenate(p_cols, axis=1), preferred_element_type=F32)
                m_ref[mp] = m_new

        def finish():
            qn = jnp.minimum(qi + 1, n_q - 1)
            stage_a(first_tile(qn), CARRY_SLOT, augmented_queries(qn))
            lam_v = lam_ref[...]
            lam = (jnp.exp(jnp.sum(lam_v[0:1] * lam_v[1:2], axis=-1, keepdims=True))
                   - jnp.exp(jnp.sum(lam_v[2:3] * lam_v[3:4], axis=-1, keepdims=True))
                   + lambda_init)
            o = (acc_ref[0, 0:LANES, :] / acc_ref[0, LANES:LANES + 1, :]
                 - lam * (acc_ref[1, 0:LANES, :] / acc_ref[1, LANES:LANES + 1, :]))
            ms = jnp.mean(o * o, axis=0, keepdims=True)
            on = (o * lax.rsqrt(ms + RMS_EPS)).T
            o_ref[0, pl.ds(pl.multiple_of(qi * tq, tq), tq), :] = (
                (on * subg_ref[...]) * (1.0 - lambda_init)).astype(o_ref.dtype)

        j0 = first_tile(qi)
        count = qi - j0
        rest = jnp.maximum(count - 1, 0)

        @pl.when(count == 0)
        def _():
            stage_b_own(CARRY_SLOT)
            finish()

        @pl.when(count > 0)
        def _():
            stage_a(j0 + 1, 0, q_aug)
            stage_b(j0, CARRY_SLOT)

        def pair_at(j):
            stage_a(j + 1, 1, q_aug)
            stage_b(j, 0)
            stage_a(j + 2, 0, q_aug)
            stage_b(j + 1, 1)

        def quad(t, c):
            j = j0 + 1 + 4 * t
            pair_at(j)
            pair_at(j + 2)
            return c

        def pair(t, c):
            pair_at(j0 + 1 + 4 * (rest // 4) + 2 * t)
            return c

        lax.fori_loop(0, rest // 4, quad, 0)
        lax.fori_loop(0, (rest % 4) // 2, pair, 0)

        @pl.when((count > 0) & (rest % 2 == 0))
        def _():
            stage_b_own(0)
            finish()

        @pl.when((count > 0) & (rest % 2 == 1))
        def _():
            stage_a(qi, 1, q_aug)
            stage_b(qi - 1, 0)
            stage_b_own(1)
            finish()

        return carry

    lax.fori_loop(0, n_q, query_tile, 0)


def _attention(qkv, slopes2, logit_bound, lam_vecs, subln_g, lambda_init, n_heads):
    bsz, s, d3 = qkv.shape
    d = d3 // 3
    assert d // n_heads == LANES
    tq = _tile(s, ATTN_TILE)
    tk = tq
    assert tk <= 2 * 256
    assert LANES % CHUNK == 0 and tq % LANES == 0

    r = jnp.arange(tk)[:, None]
    c = jnp.arange(tq)[None, :]
    dist = jnp.minimum(0, 2 * (c - r)).astype(F32)
    mask = jnp.where(r // CHUNK <= c // CHUNK, 0.0, NEG_BIG).astype(F32)

    slope_rows = jnp.broadcast_to(slopes2[:, None, None], (n_heads, 1, tq)).astype(F32)
    hi, mid, lo = (p.astype(F32) for p in _split3(slopes2))
    pieces = jnp.stack([2 * hi, 2 * mid, 2 * lo, hi, mid, lo], axis=-1)
    qbias = jnp.zeros((n_heads, 1, LANES), F32).at[:, 0, :N_BIAS_COLS].set(pieces)
    skip = jnp.ceil((F32_MIN_EXP + F32_MANT_BITS + 2.0 * logit_bound) / slopes2)
    skip = jnp.clip(skip, 1, 2 ** 30).astype(jnp.int32)

    grid_spec = pltpu.PrefetchScalarGridSpec(
        num_scalar_prefetch=1,
        grid=(bsz, n_heads),
        in_specs=[pl.BlockSpec((1, s, LANES), lambda b, h, sk: (b, 0, h)),
                  pl.BlockSpec((1, s, LANES), lambda b, h, sk: (b, 0, n_heads + h)),
                  pl.BlockSpec((1, s, LANES), lambda b, h, sk: (b, 0, 2 * n_heads + h)),
                  pl.BlockSpec((1, 1, tq), lambda b, h, sk: (h, 0, 0)),
                  pl.BlockSpec((1, 1, LANES), lambda b, h, sk: (h, 0, 0)),
                  pl.BlockSpec(lam_vecs.shape, lambda b, h, sk: (0, 0)),
                  pl.BlockSpec((1, LANES), lambda b, h, sk: (0, 0)),
                  pl.BlockSpec((tk, tq), lambda b, h, sk: (0, 0)),
                  pl.BlockSpec((tk, tq), lambda b, h, sk: (0, 0))],
        out_specs=pl.BlockSpec((1, s, LANES), lambda b, h, sk: (b, 0, h)),
        scratch_shapes=[pltpu.VMEM((s // tk, tk, 2 * LANES), BF16),
                        pltpu.VMEM((s // tk, ACC_ROWS, tk), BF16),
                        pltpu.VMEM((tk, tq), F32),
                        pltpu.VMEM((3, 2, tk, tq), F32),
                        pltpu.VMEM((3, 2, 1, tq), F32),
                        pltpu.VMEM((2, 1, tq), F32),
                        pltpu.VMEM((2, ACC_ROWS, tq), F32)])
    return pl.pallas_call(
        functools.partial(_attn_kernel, tq=tq, tk=tk, lambda_init=lambda_init),
        grid_spec=grid_spec,
        out_shape=jax.ShapeDtypeStruct((bsz, s, d), BF16),
        compiler_params=_params("parallel", "parallel"),
        name="diff_attn",
    )(skip, qkv, qkv, qkv, slope_rows, qbias, lam_vecs, subln_g.reshape(1, LANES), dist, mask)


CONV_HALO = BF16_SUBLANES


def _conv_in_kernel(x_ref, xh_ref, mod_ref, g_ref, wb_ref, wc_ref, wu_ref, cw_ref, o_ref,
                    h_ref, ext_ref):
    i = pl.program_id(1)
    n = pl.program_id(2)
    tm = x_ref.shape[1]
    halo = CONV_HALO

    def column_tile(first):
        if first:
            g, shift, scale = g_ref[0, 1:2, :], mod_ref[0, 0, 3:4, :], mod_ref[0, 0, 4:5, :]
            _norm_modulate_into(h_ref.at[0:halo], xh_ref, g, shift, scale)
            _norm_modulate_into(h_ref.at[halo:halo + tm], x_ref, g, shift, scale)
        h_all = h_ref[...]
        c = jnp.dot(h_all, wc_ref[0], preferred_element_type=F32)
        u = jnp.dot(h_all, wu_ref[0], preferred_element_type=F32)
        b = jnp.dot(h_ref[halo:halo + tm, :], wb_ref[0], preferred_element_type=F32)
        v = c * u
        ext_ref[0:halo, :] = jnp.where(i == 0, 0.0, v[0:halo])
        ext_ref[halo:, :] = v[halo:]
        v1 = ext_ref[halo - 1:halo - 1 + tm, :]
        v2 = ext_ref[halo - 2:halo - 2 + tm, :]
        w = cw_ref[0]
        y = w[0:1] * v2 + w[1:2] * v1 + w[2:3] * v[halo:]
        o_ref[0] = (b * y).astype(o_ref.dtype)

    pl.when(n == 0)(lambda: column_tile(True))
    pl.when(n > 0)(lambda: column_tile(False))


def _conv_in(x, mod, norm_g, w_in, conv_w, layer, slot):
    bsz, s, d = x.shape
    assert CONV_WIDTH - 1 <= CONV_HALO
    tm = _tile(s, 1024)
    tn = _tile(d, 512)
    nc = d // tn
    hb = tm // CONV_HALO
    return pl.pallas_call(
        _conv_in_kernel,
        grid=(bsz, s // tm, nc),
        in_specs=[pl.BlockSpec((1, tm, d), lambda b, i, j: (b, i, 0)),
                  pl.BlockSpec((1, CONV_HALO, d),
                               lambda b, i, j: (b, jnp.maximum(i * hb - 1, 0), 0)),
                  pl.BlockSpec((1, 1, N_MOD, d), lambda b, i, j: (layer, b, 0, 0)),
                  pl.BlockSpec((1, 3, d), lambda b, i, j: (layer, 0, 0)),
                  pl.BlockSpec((1, d, tn), lambda b, i, j: (slot, 0, j)),
                  pl.BlockSpec((1, d, tn), lambda b, i, j: (slot, 0, nc + j)),
                  pl.BlockSpec((1, d, tn), lambda b, i, j: (slot, 0, 2 * nc + j)),
                  pl.BlockSpec((1, CONV_WIDTH, tn), lambda b, i, j: (slot, 0, j))],
        out_specs=pl.BlockSpec((1, tm, tn), lambda b, i, j: (b, i, j)),
        out_shape=jax.ShapeDtypeStruct((bsz, s, d), BF16),
        scratch_shapes=[pltpu.VMEM((CONV_HALO + tm, d), BF16),
                        pltpu.VMEM((CONV_HALO + tm, tn), F32)],
        compiler_params=_params("parallel", "parallel", "arbitrary"),
        name="conv_in",
    )(x, x, mod, norm_g, w_in, w_in, w_in, conv_w)


HGRN_CHUNKS_PER_STEP = 16
HGRN_TIME_TILE = 8192


def _hgrn_kernel(q_ref, fl_ref, i_ref, g_ref, lbl_ref, ong_ref, o_ref,
                 state_ref, b_ref, *, layer, ts):
    t = pl.program_id(2)
    c64 = CHUNK
    sb = SUBLANES

    @pl.when(t == 0)
    def _():
        state_ref[...] = jnp.zeros_like(state_ref)

    lbl = lbl_ref[...]
    e = jnp.exp(lbl - jnp.max(lbl, axis=0, keepdims=True))
    sm = e / jnp.sum(e, axis=0, keepdims=True)
    lb = jnp.zeros((1, LANES), F32)
    for r in range(1, layer + 1):
        lb = lb + sm[r:r + 1]

    row = lax.broadcasted_iota(jnp.int32, (c64, c64), 0)
    col = lax.broadcasted_iota(jnp.int32, (c64, c64), 1)
    tri = jnp.where(row >= col, 1.0, 0.0).astype(BF16)
    ones = jnp.ones((LANES, LANES), BF16)
    sub_row = lax.broadcasted_iota(jnp.int32, (sb, LANES), 0)
    col8 = lax.broadcasted_iota(jnp.int32, (sb, c64), 1)
    groups = (2 * sb, 4 * sb, 8 * sb)
    level_masks = [(row // grp == col // grp) & (row % grp >= grp // 2) & (col % grp < grp // 2)
                   for grp in groups]


    def gates_and_cumsum(r0, u):
        fl = fl_ref[0, pl.ds(r0, c64), :]
        f = lb + (1.0 - lb) * jax.nn.sigmoid(fl)
        lf = jnp.log2(f)
        b = jnp.zeros((c64, LANES), F32)
        for piece in _split3(lf):
            b = b + jnp.dot(tri, piece, preferred_element_type=F32)
        b_ref[u] = b
        return dict(q=q_ref[0, pl.ds(r0, c64), :], kk=1.0 - f, b=b,
                    i16=i_ref[0, pl.ds(r0, c64), :].astype(BF16))

    def diagonal_sums(ch, u):
        w_rows = []
        for blk in range(c64 // sb):
            b_blk = ch["b"][blk * sb:(blk + 1) * sb]
            q_blk = ch["q"][blk * sb:(blk + 1) * sb]
            for s_ in range(sb):
                src = blk * sb + s_
                dec = jnp.exp2(b_blk - b_ref[u, src:src + 1, :])
                w_rows.append(jnp.where(sub_row >= s_,
                                        q_blk * dec * ch["kk"][src:src + 1, :], 0.0))
        w_all = jnp.concatenate(w_rows, axis=0).astype(BF16)
        ch["sums"] = jnp.dot(w_all, ones, preferred_element_type=F32)

    def level_scores(ch, u):
        q, kk, b = ch["q"], ch["kk"], ch["b"]
        out = []
        for grp in groups:
            parts = []
            for gidx in range(c64 // grp):
                edge = gidx * grp + grp // 2 - 1
                parts.append(jnp.broadcast_to(b_ref[u, edge:edge + 1, :], (grp, LANES)))
            ref = parts[0] if len(parts) == 1 else jnp.concatenate(parts, axis=0)
            qd = (q * jnp.exp2(jnp.minimum(b - ref, 0.0))).astype(BF16)
            kd = (kk * jnp.exp2(jnp.minimum(ref - b, 0.0))).astype(BF16)
            out.append(lax.dot_general(qd, kd, NT_DIMS, preferred_element_type=F32))
        ch["levels"] = out
        b_last = b_ref[u, c64 - 1:c64, :]
        kd_end = (kk * jnp.exp2(b_last - b)).astype(BF16)
        ch["upd"] = lax.dot_general(ch["i16"], kd_end, TN_DIMS,
                                    preferred_element_type=F32)
        ch["decay"] = jnp.exp2(b_last)
        ch["q_in"] = (q * jnp.exp2(b)).astype(BF16)

    def intra_chunk(ch):
        score_rows = []
        for blk in range(c64 // sb):
            acc = jnp.zeros((sb, c64), F32)
            for s_ in range(sb):
                idx = blk * sb + s_
                acc = jnp.where(col8 == idx, ch["sums"][idx * sb:(idx + 1) * sb, :c64], acc)
            score_rows.append(acc)
        scores = jnp.concatenate(score_rows, axis=0)
        for lvl in range(len(groups)):
            scores = scores + jnp.where(level_masks[lvl], ch["levels"][lvl], 0.0)
        ch["intra"] = jnp.dot(scores.astype(BF16), ch["i16"], preferred_element_type=F32)

    per_step = b_ref.shape[0]

    def step(c, carry):
        base = c * (per_step * c64)
        starts = [pl.multiple_of(base + u * c64, c64) for u in range(per_step)]
        chunks = [gates_and_cumsum(r0, u) for u, r0 in enumerate(starts)]
        for u, ch in enumerate(chunks):
            diagonal_sums(ch, u)
        for u, ch in enumerate(chunks):
            level_scores(ch, u)
        for ch in chunks:
            intra_chunk(ch)
        st = state_ref[...]
        for r0, ch in zip(starts, chunks):
            inter = lax.dot_general(ch["q_in"], st.astype(BF16), NT_DIMS,
                                    preferred_element_type=F32)
            st = st * ch["decay"] + ch["upd"]
            o = inter + ch["intra"]
            ms = jnp.mean(o * o, axis=-1, keepdims=True)
            on = (o * lax.rsqrt(ms + RMS_EPS)) * ong_ref[...]
            gv = g_ref[0, pl.ds(r0, c64), :]
            o_ref[0, pl.ds(r0, c64), :] = (on * (gv * jax.nn.sigmoid(gv))).astype(o_ref.dtype)
        state_ref[...] = st
        return carry

    lax.fori_loop(0, ts // (per_step * c64), step, 0)


def _hgrn_core(proj, lb_logits, o_norm_g, layer, n_heads):
    bsz, s, d4 = proj.shape
    d = d4 // 4
    assert d // n_heads == LANES
    ts = _tile(s, HGRN_TIME_TILE)
    per_step = math.gcd(HGRN_CHUNKS_PER_STEP, ts // CHUNK)
    depth = lb_logits.shape[0]
    return pl.pallas_call(
        functools.partial(_hgrn_kernel, layer=layer, ts=ts),
        grid=(bsz, n_heads, s // ts),
        in_specs=[pl.BlockSpec((1, ts, LANES), lambda b, h, t: (b, t, h)),
                  pl.BlockSpec((1, ts, LANES), lambda b, h, t: (b, t, n_heads + h)),
                  pl.BlockSpec((1, ts, LANES), lambda b, h, t: (b, t, 2 * n_heads + h)),
                  pl.BlockSpec((1, ts, LANES), lambda b, h, t: (b, t, 3 * n_heads + h)),
                  pl.BlockSpec((depth, LANES), lambda b, h, t: (0, h)),
                  pl.BlockSpec((1, LANES), lambda b, h, t: (0, 0))],
        out_specs=pl.BlockSpec((1, ts, LANES), lambda b, h, t: (b, t, h)),
        out_shape=jax.ShapeDtypeStruct((bsz, s, d), BF16),
        scratch_shapes=[pltpu.VMEM((LANES, LANES), F32),
                        pltpu.VMEM((per_step, CHUNK, LANES), F32)],
        compiler_params=_params("parallel", "parallel", "arbitrary"),
        name="hgrn2",
    )(proj, proj, proj, proj, lb_logits, o_norm_g.reshape(1, LANES))


def kernel(x, c, ada_w, ada_b, norm_g, ffn_w_gate, ffn_w_up, ffn_w_down,
           attn_w_in, attn_w_out, attn_q_gain, attn_k_gain, attn_lambda, attn_subln_g,
           conv_w_in, conv_w, conv_w_out,
           hgrn_w_in, hgrn_w_out, hgrn_o_norm_g, hgrn_lb_logits):
    depth = ada_w.shape[0]
    d = x.shape[-1]
    qk_dim = attn_q_gain.shape[-1]
    attn_heads = d // (2 * qk_dim)
    hgrn_heads = d // hgrn_o_norm_g.shape[-1]
    ffn_w_gate, ffn_w_up, ffn_w_down = (w.astype(BF16) for w in (ffn_w_gate, ffn_w_up, ffn_w_down))
    attn_w_in, attn_w_out = attn_w_in.astype(BF16), attn_w_out.astype(BF16)
    conv_w_in, conv_w_out = conv_w_in.astype(BF16), conv_w_out.astype(BF16)
    hgrn_w_in, hgrn_w_out = hgrn_w_in.astype(BF16), hgrn_w_out.astype(BF16)

    mod = _ada_mod(c, ada_w, ada_b)
    slopes2 = jnp.asarray(
        [2.0 ** (-8.0 * (h + 1) / attn_heads) * LOG2E for h in range(attn_heads)], F32)
    q_scale = qk_dim ** -0.5 * LOG2E

    for layer in range(depth):
        x = _ffn(x, mod, norm_g, ffn_w_gate, ffn_w_up, ffn_w_down, layer, 0)
        kind, slot = layer % N_MIXERS, layer // N_MIXERS
        if kind == 0:
            lambda_init = 0.8 - 0.6 * math.exp(-0.3 * layer)
            gains = jnp.stack([attn_q_gain[slot], attn_k_gain[slot]])
            qkv = _proj_in(x, mod, norm_g, attn_w_in, layer, slot, BF16,
                           gains=gains, q_scale=q_scale)
            logit_bound = (1.02 * qk_dim * q_scale * jnp.max(jnp.abs(attn_q_gain[slot]))
                           * jnp.max(jnp.abs(attn_k_gain[slot])))
            y = _attention(qkv, slopes2, logit_bound, attn_lambda[slot], attn_subln_g[slot],
                           lambda_init, attn_heads)
            x = _proj_out(y, attn_w_out, x, mod, layer, slot)
        elif kind == 1:
            y = _conv_in(x, mod, norm_g, conv_w_in, conv_w, layer, slot)
            x = _proj_out(y, conv_w_out, x, mod, layer, slot)
        else:
            proj = _proj_in(x, mod, norm_g, hgrn_w_in, layer, slot, F32)
            y = _hgrn_core(proj, hgrn_lb_logits, hgrn_o_norm_g[slot], layer, hgrn_heads)
            x = _proj_out(y, hgrn_w_out, x, mod, layer, slot)
        x = _ffn(x, mod, norm_g, ffn_w_gate, ffn_w_up, ffn_w_down, layer, 1)
    return x
```

```python
import functools
import math

import jax
import jax.numpy as jnp
from jax import lax
from jax.experimental import pallas as pl
from jax.experimental.pallas import tpu as pltpu

F32 = jnp.float32
BF16 = jnp.bfloat16

RMS_EPS = 1e-6
CHUNK = 64
N_MIXERS = 3
N_MOD = 9
CONV_WIDTH = 3
LOG2E = 1.4426950408889634
NEG_BIG = -1e30
F32_MIN_EXP = 127
F32_MANT_BITS = 24

LANES = 128
SUBLANES = 8
BF16_SUBLANES = 16
NORM_ROW_BLOCK = 16
V7X_VMEM_LIMIT_BYTES = 60 * 1024 * 1024

NT_DIMS = (((1,), (1,)), ((), ()))
TN_DIMS = (((0,), (0,)), ((), ()))


def _tile(n, preferred):
    for t in (preferred, 2048, 1024, 512, 256, 128, 64, 32, 16, 8):
        if t <= preferred and n % t == 0:
            return t
    return n


def _params(*sem):
    return pltpu.CompilerParams(dimension_semantics=sem,
                                vmem_limit_bytes=V7X_VMEM_LIMIT_BYTES)


def _norm_modulate_into(h_ref, x_ref, g, shift, scale):
    gain = g * (1.0 + scale)
    rows = h_ref.shape[0]
    blk = min(rows, NORM_ROW_BLOCK)
    for r in range(0, rows, blk):
        x = x_ref[0, r:r + blk, :]
        ms = jnp.mean(x * x, axis=-1, keepdims=True)
        h_ref[r:r + blk, :] = ((x * lax.rsqrt(ms + RMS_EPS)) * gain + shift).astype(BF16)


def _split3(x):
    hi = x.astype(BF16)
    r1 = x - hi.astype(F32)
    mid = r1.astype(BF16)
    lo = (r1 - mid.astype(F32)).astype(BF16)
    return hi, mid, lo


def _ada_kernel(c_ref, w_ref, b_ref, o_ref):
    c = c_ref[...]
    cond = (c * jax.nn.sigmoid(c)).astype(BF16)
    o_ref[0] = jnp.dot(cond, w_ref[0].astype(BF16),
                       preferred_element_type=F32) + b_ref[0]


def _ada_mod(c, ada_w, ada_b):
    depth, d, n = ada_w.shape
    bsz = c.shape[0]
    rows = -(-bsz // SUBLANES) * SUBLANES
    c_pad = jnp.zeros((rows, d), F32).at[:bsz].set(c)
    tn = _tile(n, 1024)
    out = pl.pallas_call(
        _ada_kernel,
        grid=(depth, n // tn),
        in_specs=[pl.BlockSpec((rows, d), lambda l, j: (0, 0)),
                  pl.BlockSpec((1, d, tn), lambda l, j: (l, 0, j)),
                  pl.BlockSpec((1, 1, tn), lambda l, j: (l, 0, j))],
        out_specs=pl.BlockSpec((1, rows, tn), lambda l, j: (l, 0, j)),
        out_shape=jax.ShapeDtypeStruct((depth, rows, n), F32),
        compiler_params=_params("parallel", "parallel"),
        name="ada_mod",
    )(c_pad, ada_w, ada_b.reshape(depth, 1, n))
    return out[:, :bsz].reshape(depth, bsz, N_MOD, d)


FFN_ROW_TILE = 1024
FFN_COL_TILE = 512


def _ffn_kernel(x_ref, mod_ref, g_ref, wg_ref, wu_ref, wd_ref, o_ref, h_ref, *, sub, row_groups,
                n_f):
    f = pl.program_id(2)
    r = 3 * sub

    @pl.when(f == 0)
    def _():
        _norm_modulate_into(h_ref, x_ref, g_ref[0, sub:sub + 1, :],
                            mod_ref[0, 0, r:r + 1, :], mod_ref[0, 0, r + 1:r + 2, :])

    rows = h_ref.shape[0] // row_groups
    slices = [slice(grp * rows, (grp + 1) * rows) for grp in range(row_groups)]

    def step(first, last):
        gated = []
        for sl in slices:
            h = h_ref[sl, :]
            a = jnp.dot(h, wg_ref[0, 0], preferred_element_type=F32)
            b = jnp.dot(h, wu_ref[0, 0], preferred_element_type=F32)
            gated.append((a * jax.nn.sigmoid(a) * b).astype(BF16))
        for sl, p in zip(slices, gated):
            total = jnp.dot(p, wd_ref[0, 0], preferred_element_type=F32)
            if not first:
                total = o_ref[0, sl, :] + total
            if last:
                total = x_ref[0, sl, :] + (0.5 * mod_ref[0, 0, r + 2:r + 3, :]) * total
            o_ref[0, sl, :] = total

    if n_f == 1:
        step(True, True)
    else:
        pl.when(f == 0)(lambda: step(True, False))
        if n_f > 2:
            pl.when((f > 0) & (f < n_f - 1))(lambda: step(False, False))
        pl.when(f == n_f - 1)(lambda: step(False, True))


def _ffn(x, mod, norm_g, w_gate, w_up, w_down, layer, which):
    bsz, s, d = x.shape
    f = w_gate.shape[-1]
    tm = _tile(s, FFN_ROW_TILE)
    tf = _tile(f, FFN_COL_TILE)
    return pl.pallas_call(
        functools.partial(_ffn_kernel, sub=2 * which, row_groups=2 if tm % 256 == 0 else 1,
                          n_f=f // tf),
        grid=(bsz, s // tm, f // tf),
        in_specs=[pl.BlockSpec((1, tm, d), lambda b, i, j: (b, i, 0)),
                  pl.BlockSpec((1, 1, N_MOD, d), lambda b, i, j: (layer, b, 0, 0)),
                  pl.BlockSpec((1, 3, d), lambda b, i, j: (layer, 0, 0)),
                  pl.BlockSpec((1, 1, d, tf), lambda b, i, j: (layer, which, 0, j)),
                  pl.BlockSpec((1, 1, d, tf), lambda b, i, j: (layer, which, 0, j)),
                  pl.BlockSpec((1, 1, tf, d), lambda b, i, j: (layer, which, j, 0))],
        out_specs=pl.BlockSpec((1, tm, d), lambda b, i, j: (b, i, 0)),
        out_shape=jax.ShapeDtypeStruct((bsz, s, d), F32),
        scratch_shapes=[pltpu.VMEM((tm, d), BF16)],
        compiler_params=_params("parallel", "parallel", "arbitrary"),
        name="ffn",
    )(x, mod, norm_g, w_gate, w_up, w_down)


def _qk_norm_store(acc, gain, group_mean, o_ref, post_scale):
    width = group_mean.shape[0]
    if post_scale != 1.0:
        gain = gain * post_scale
    for j in range(acc.shape[1] // width):
        sl = slice(j * width, (j + 1) * width)
        y = acc[:, sl]
        y2 = y * y
        y2_hi = y2.astype(BF16)
        y2_lo = (y2 - y2_hi.astype(F32)).astype(BF16)
        ms = (jnp.dot(y2_hi, group_mean, preferred_element_type=F32)
              + jnp.dot(y2_lo, group_mean, preferred_element_type=F32))
        yn = (y * lax.rsqrt(ms + RMS_EPS)) * gain[:, sl]
        o_ref[0, :, sl] = yn.astype(o_ref.dtype)


def _proj_in_kernel(x_ref, mod_ref, g_ref, w_ref, *rest, qk_norm, q_scale, tiles_per_section):
    if qk_norm:
        gains_ref, gmean_ref, o_ref, h_ref = rest
    else:
        o_ref, h_ref = rest
    n = pl.program_id(2)

    def column_tile(first):
        if first:
            _norm_modulate_into(h_ref, x_ref, g_ref[0, 1:2, :],
                                mod_ref[0, 0, 3:4, :], mod_ref[0, 0, 4:5, :])
        acc = jnp.dot(h_ref[...], w_ref[0], preferred_element_type=F32)
        if not qk_norm:
            o_ref[0] = acc.astype(o_ref.dtype)
        elif first:
            _qk_norm_store(acc, gains_ref[0:1, :], gmean_ref[...], o_ref, q_scale)
        else:
            section = n // tiles_per_section

            @pl.when(section == 0)
            def _():
                _qk_norm_store(acc, gains_ref[0:1, :], gmean_ref[...], o_ref, q_scale)

            @pl.when(section == 1)
            def _():
                _qk_norm_store(acc, gains_ref[1:2, :], gmean_ref[...], o_ref, 1.0)

            @pl.when(section == 2)
            def _():
                o_ref[0] = acc.astype(o_ref.dtype)

    pl.when(n == 0)(lambda: column_tile(True))
    pl.when(n > 0)(lambda: column_tile(False))


def _proj_in(x, mod, norm_g, w, layer, slot, out_dtype, gains=None, q_scale=1.0):
    bsz, s, d = x.shape
    n = w.shape[-1]
    tm = _tile(s, 1024)
    tn = _tile(d, 1024)
    qk_norm = gains is not None
    in_specs = [pl.BlockSpec((1, tm, d), lambda b, i, j: (b, i, 0)),
                pl.BlockSpec((1, 1, N_MOD, d), lambda b, i, j: (layer, b, 0, 0)),
                pl.BlockSpec((1, 3, d), lambda b, i, j: (layer, 0, 0)),
                pl.BlockSpec((1, d, tn), lambda b, i, j: (slot, 0, j))]
    args = [x, mod, norm_g, w]
    if qk_norm:
        group = gains.shape[1]
        width = _tile(tn, 2 * LANES)
        blk = jnp.arange(width) // group
        gmean = jnp.where(blk[:, None] == blk[None, :], 1.0 / group, 0.0).astype(BF16)
        in_specs += [pl.BlockSpec((2, tn), lambda b, i, j: (0, 0)),
                     pl.BlockSpec((width, width), lambda b, i, j: (0, 0))]
        args += [jnp.tile(gains, (1, tn // group)), gmean]
    return pl.pallas_call(
        functools.partial(_proj_in_kernel, qk_norm=qk_norm, q_scale=q_scale,
                          tiles_per_section=d // tn),
        grid=(bsz, s // tm, n // tn),
        in_specs=in_specs,
        out_specs=pl.BlockSpec((1, tm, tn), lambda b, i, j: (b, i, j)),
        out_shape=jax.ShapeDtypeStruct((bsz, s, n), out_dtype),
        scratch_shapes=[pltpu.VMEM((tm, d), BF16)],
        compiler_params=_params("parallel", "parallel", "arbitrary"),
        name="proj_in",
    )(*args)


def _proj_out_kernel(y_ref, w_ref, x_ref, mod_ref, o_ref):
    o_ref[0] = x_ref[0] + mod_ref[0, 0, 5:6, :] * jnp.dot(
        y_ref[0], w_ref[0], preferred_element_type=F32)


def _proj_out(y, w, x, mod, layer, slot):
    bsz, s, d = x.shape
    k = y.shape[2]
    tm = _tile(s, 512)
    return pl.pallas_call(
        _proj_out_kernel,
        grid=(bsz, s // tm),
        in_specs=[pl.BlockSpec((1, tm, k), lambda b, i: (b, i, 0)),
                  pl.BlockSpec((1, k, d), lambda b, i: (slot, 0, 0)),
                  pl.BlockSpec((1, tm, d), lambda b, i: (b, i, 0)),
                  pl.BlockSpec((1, 1, N_MOD, d), lambda b, i: (layer, b, 0, 0))],
        out_specs=pl.BlockSpec((1, tm, d), lambda b, i: (b, i, 0)),
        out_shape=jax.ShapeDtypeStruct((bsz, s, d), F32),
        compiler_params=_params("parallel", "parallel"),
        name="proj_out",
    )(y, w, x, mod)


ACC_ROWS = LANES + BF16_SUBLANES
N_BIAS_COLS = 6
ATTN_TILE = 512
CARRY_SLOT = 2


def _attn_kernel(skip_ref, q_ref, k_ref, v_ref, slope_ref, qbias_ref, lam_ref, subg_ref,
                 dist_ref, mask_ref, o_ref, kaug_ref, vt_ref, own_ref, s_ref, mt_ref, m_ref,
                 acc_ref, *, tq, tk, lambda_init):
    h = pl.program_id(1)
    n_kv = vt_ref.shape[0]
    n_q = q_ref.shape[1] // tq

    own_ref[...] = slope_ref[0] * dist_ref[...] + mask_ref[...]
    row16 = lax.broadcasted_iota(jnp.int32, (BF16_SUBLANES, tk), 0)
    ones_rows = jnp.where(row16 == 0, 1.0, 0.0).astype(BF16)
    lane = lax.broadcasted_iota(jnp.int32, (tk, LANES), 1)
    row = lax.broadcasted_iota(jnp.int32, (tk, LANES), 0)
    pos = jnp.where(lane < N_BIAS_COLS // 2, row >> 1,
                    jnp.where(lane < N_BIAS_COLS, row & 1, 0)).astype(F32).astype(BF16)
    for c in range(n_kv):
        blk = v_ref[0, c * tk:(c + 1) * tk, :].astype(F32)
        vt_ref[c, 0:LANES, :] = blk.T.astype(BF16)
        vt_ref[c, LANES:ACC_ROWS, :] = ones_rows
        kaug_ref[c, :, 0:LANES] = k_ref[0, c * tk:(c + 1) * tk, :]
        kaug_ref[c, :, LANES:2 * LANES] = pos

    lo = lax.broadcasted_iota(jnp.int32, (1, LANES), 1) < LANES // 2
    q_bias = jnp.broadcast_to(qbias_ref[0], (tq, LANES)).astype(BF16)
    slope2 = slope_ref[0]

    def augmented_queries(qt):
        start = qt * tq
        if not isinstance(start, int):
            start = pl.multiple_of(start, tq)
        q = q_ref[0, pl.ds(start, tq), :]
        zero = jnp.zeros_like(q)
        return (jnp.concatenate([jnp.where(lo, q, zero), q_bias], axis=1),
                jnp.concatenate([jnp.where(lo, zero, q), q_bias], axis=1))

    def first_tile(qt):
        return jnp.maximum(qt * tq - skip_ref[h] + 1, 0) // tk

    def stage_a(j, slot, q_pair):
        kt = kaug_ref[j]
        for mp in range(2):
            s = lax.dot_general(kt, q_pair[mp], NT_DIMS, preferred_element_type=F32)
            s_ref[slot, mp] = s
            mt_ref[slot, mp] = jnp.max(s, axis=0, keepdims=True)

    def accumulate(mp, z, tile_max, shift, vt):
        m_old = m_ref[mp]
        m_new = jnp.maximum(m_old, tile_max + shift)
        alpha = jnp.exp2(m_old - m_new)
        p = jnp.exp2(z - (m_new - shift)).astype(BF16)
        acc_ref[mp] = alpha * acc_ref[mp] + jnp.dot(vt, p, preferred_element_type=F32)
        m_ref[mp] = m_new

    stage_a(0, CARRY_SLOT, augmented_queries(0))

    def query_tile(qi, carry):
        q_aug = augmented_queries(qi)
        acc_ref[...] = jnp.zeros_like(acc_ref)
        m_ref[...] = jnp.full(m_ref.shape, NEG_BIG, F32)

        def stage_b(j, slot):
            shift = slope2 * lax.convert_element_type(j * tk - qi * tq, F32)
            vt = vt_ref[j]
            for mp in range(2):
                accumulate(mp, s_ref[slot, mp], mt_ref[slot, mp], shift, vt)

        def stage_b_own(slot):
            vt = vt_ref[qi]
            for mp in range(2):
                m_old = m_ref[mp]
                m_cols, p_cols = [], []
                for cb in range(tq // LANES):
                    cols = slice(cb * LANES, (cb + 1) * LANES)
                    rows = min((cb + 1) * LANES, tk)
                    z = s_ref[slot, mp, 0:rows, cols] + own_ref[0:rows, cols]
                    m_new = jnp.maximum(m_old[:, cols], jnp.max(z, axis=0, keepdims=True))
                    p = jnp.exp2(z - m_new).astype(BF16)
                    if rows < tk:
                        p = jnp.concatenate([p, jnp.zeros((tk - rows, LANES), BF16)], axis=0)
                    m_cols.append(m_new)
                    p_cols.append(p)
                m_new = jnp.concatenate(m_cols, axis=1)
                alpha = jnp.exp2(m_old - m_new)
                acc_ref[mp] = alpha * acc_ref[mp] + jnp.dot(
                    vt, jnp.concatenate(p_cols, axis=1), preferred_element_type=F32)
                m_ref[mp] = m_new

        def finish():
            qn = jnp.minimum(qi + 1, n_q - 1)
            stage_a(first_tile(qn), CARRY_SLOT, augmented_queries(qn))
            lam_v = lam_ref[...]
            lam = (jnp.exp(jnp.sum(lam_v[0:1] * lam_v[1:2], axis=-1, keepdims=True))
                   - jnp.exp(jnp.sum(lam_v[2:3] * lam_v[3:4], axis=-1, keepdims=True))
                   + lambda_init)
            o = (acc_ref[0, 0:LANES, :] / acc_ref[0, LANES:LANES + 1, :]
                 - lam * (acc_ref[1, 0:LANES, :] / acc_ref[1, LANES:LANES + 1, :]))
            ms = jnp.mean(o * o, axis=0, keepdims=True)
            on = (o * lax.rsqrt(ms + RMS_EPS)).T
            o_ref[0, pl.ds(pl.multiple_of(qi * tq, tq), tq), :] = (
                (on * subg_ref[...]) * (1.0 - lambda_init)).astype(o_ref.dtype)

        j0 = first_tile(qi)
        count = qi - j0
        rest = jnp.maximum(count - 1, 0)

        @pl.when(count == 0)
        def _():
            stage_b_own(CARRY_SLOT)
            finish()

        @pl.when(count > 0)
        def _():
            stage_a(j0 + 1, 0, q_aug)
            stage_b(j0, CARRY_SLOT)

        def pair_at(j):
            stage_a(j + 1, 1, q_aug)
            stage_b(j, 0)
            stage_a(j + 2, 0, q_aug)
            stage_b(j + 1, 1)

        def tiles_per_trip(n, first):
            def body(t, c):
                for k in range(0, n, 2):
                    pair_at(first + n * t + k)
                return c
            return body

        after8 = j0 + 1 + 8 * (rest // 8)
        after4 = after8 + 4 * ((rest % 8) // 4)
        lax.fori_loop(0, rest // 8, tiles_per_trip(8, j0 + 1), 0)
        lax.fori_loop(0, (rest % 8) // 4, tiles_per_trip(4, after8), 0)
        lax.fori_loop(0, (rest % 4) // 2, tiles_per_trip(2, after4), 0)

        @pl.when((count > 0) & (rest % 2 == 0))
        def _():
            stage_b_own(0)
            finish()

        @pl.when((count > 0) & (rest % 2 == 1))
        def _():
            stage_a(qi, 1, q_aug)
            stage_b(qi - 1, 0)
            stage_b_own(1)
            finish()

        return carry

    lax.fori_loop(0, n_q, query_tile, 0)


def _attention(qkv, slopes2, logit_bound, lam_vecs, subln_g, lambda_init, n_heads):
    bsz, s, d3 = qkv.shape
    d = d3 // 3
    assert d // n_heads == LANES
    tq = _tile(s, ATTN_TILE)
    tk = tq
    assert tk <= 2 * 256
    assert LANES % CHUNK == 0 and tq % LANES == 0

    r = jnp.arange(tk)[:, None]
    c = jnp.arange(tq)[None, :]
    dist = jnp.minimum(0, 2 * (c - r)).astype(F32)
    mask = jnp.where(r // CHUNK <= c // CHUNK, 0.0, NEG_BIG).astype(F32)

    slope_rows = jnp.broadcast_to(slopes2[:, None, None], (n_heads, 1, tq)).astype(F32)
    hi, mid, lo = (p.astype(F32) for p in _split3(slopes2))
    pieces = jnp.stack([2 * hi, 2 * mid, 2 * lo, hi, mid, lo], axis=-1)
    qbias = jnp.zeros((n_heads, 1, LANES), F32).at[:, 0, :N_BIAS_COLS].set(pieces)
    skip = jnp.ceil((F32_MIN_EXP + F32_MANT_BITS + 2.0 * logit_bound) / slopes2)
    skip = jnp.clip(skip, 1, 2 ** 30).astype(jnp.int32)

    grid_spec = pltpu.PrefetchScalarGridSpec(
        num_scalar_prefetch=1,
        grid=(bsz, n_heads),
        in_specs=[pl.BlockSpec((1, s, LANES), lambda b, h, sk: (b, 0, h)),
                  pl.BlockSpec((1, s, LANES), lambda b, h, sk: (b, 0, n_heads + h)),
                  pl.BlockSpec((1, s, LANES), lambda b, h, sk: (b, 0, 2 * n_heads + h)),
                  pl.BlockSpec((1, 1, tq), lambda b, h, sk: (h, 0, 0)),
                  pl.BlockSpec((1, 1, LANES), lambda b, h, sk: (h, 0, 0)),
                  pl.BlockSpec(lam_vecs.shape, lambda b, h, sk: (0, 0)),
                  pl.BlockSpec((1, LANES), lambda b, h, sk: (0, 0)),
                  pl.BlockSpec((tk, tq), lambda b, h, sk: (0, 0)),
                  pl.BlockSpec((tk, tq), lambda b, h, sk: (0, 0))],
        out_specs=pl.BlockSpec((1, s, LANES), lambda b, h, sk: (b, 0, h)),
        scratch_shapes=[pltpu.VMEM((s // tk, tk, 2 * LANES), BF16),
                        pltpu.VMEM((s // tk, ACC_ROWS, tk), BF16),
                        pltpu.VMEM((tk, tq), F32),
                        pltpu.VMEM((3, 2, tk, tq), F32),
                        pltpu.VMEM((3, 2, 1, tq), F32),
                        pltpu.VMEM((2, 1, tq), F32),
                        pltpu.VMEM((2, ACC_ROWS, tq), F32)])
    return pl.pallas_call(
        functools.partial(_attn_kernel, tq=tq, tk=tk, lambda_init=lambda_init),
        grid_spec=grid_spec,
        out_shape=jax.ShapeDtypeStruct((bsz, s, d), BF16),
        compiler_params=_params("parallel", "parallel"),
        name="diff_attn",
    )(skip, qkv, qkv, qkv, slope_rows, qbias, lam_vecs, subln_g.reshape(1, LANES), dist, mask)


CONV_HALO = BF16_SUBLANES


def _conv_in_kernel(x_ref, xh_ref, mod_ref, g_ref, wb_ref, wc_ref, wu_ref, cw_ref, o_ref,
                    h_ref, ext_ref):
    i = pl.program_id(1)
    n = pl.program_id(2)
    tm = x_ref.shape[1]
    halo = CONV_HALO

    def column_tile(first):
        if first:
            g, shift, scale = g_ref[0, 1:2, :], mod_ref[0, 0, 3:4, :], mod_ref[0, 0, 4:5, :]
            _norm_modulate_into(h_ref.at[0:halo], xh_ref, g, shift, scale)
            _norm_modulate_into(h_ref.at[halo:halo + tm], x_ref, g, shift, scale)
        h_all = h_ref[...]
        c = jnp.dot(h_all, wc_ref[0], preferred_element_type=F32)
        u = jnp.dot(h_all, wu_ref[0], preferred_element_type=F32)
        b = jnp.dot(h_ref[halo:halo + tm, :], wb_ref[0], preferred_element_type=F32)
        v = c * u
        ext_ref[0:halo, :] = jnp.where(i == 0, 0.0, v[0:halo])
        ext_ref[halo:, :] = v[halo:]
        v1 = ext_ref[halo - 1:halo - 1 + tm, :]
        v2 = ext_ref[halo - 2:halo - 2 + tm, :]
        w = cw_ref[0]
        y = w[0:1] * v2 + w[1:2] * v1 + w[2:3] * v[halo:]
        o_ref[0] = (b * y).astype(o_ref.dtype)

    pl.when(n == 0)(lambda: column_tile(True))
    pl.when(n > 0)(lambda: column_tile(False))


def _conv_in(x, mod, norm_g, w_in, conv_w, layer, slot):
    bsz, s, d = x.shape
    assert CONV_WIDTH - 1 <= CONV_HALO
    tm = _tile(s, 1024)
    tn = _tile(d, 512)
    nc = d // tn
    hb = tm // CONV_HALO
    return pl.pallas_call(
        _conv_in_kernel,
        grid=(bsz, s // tm, nc),
        in_specs=[pl.BlockSpec((1, tm, d), lambda b, i, j: (b, i, 0)),
                  pl.BlockSpec((1, CONV_HALO, d),
                               lambda b, i, j: (b, jnp.maximum(i * hb - 1, 0), 0)),
                  pl.BlockSpec((1, 1, N_MOD, d), lambda b, i, j: (layer, b, 0, 0)),
                  pl.BlockSpec((1, 3, d), lambda b, i, j: (layer, 0, 0)),
                  pl.BlockSpec((1, d, tn), lambda b, i, j: (slot, 0, j)),
                  pl.BlockSpec((1, d, tn), lambda b, i, j: (slot, 0, nc + j)),
                  pl.BlockSpec((1, d, tn), lambda b, i, j: (slot, 0, 2 * nc + j)),
                  pl.BlockSpec((1, CONV_WIDTH, tn), lambda b, i, j: (slot, 0, j))],
        out_specs=pl.BlockSpec((1, tm, tn), lambda b, i, j: (b, i, j)),
        out_shape=jax.ShapeDtypeStruct((bsz, s, d), BF16),
        scratch_shapes=[pltpu.VMEM((CONV_HALO + tm, d), BF16),
                        pltpu.VMEM((CONV_HALO + tm, tn), F32)],
        compiler_params=_params("parallel", "parallel", "arbitrary"),
        name="conv_in",
    )(x, x, mod, norm_g, w_in, w_in, w_in, conv_w)


HGRN_CHUNKS_PER_STEP = 32
HGRN_TIME_TILE = 8192


def _hgrn_kernel(q_ref, fl_ref, i_ref, g_ref, lbl_ref, ong_ref, o_ref,
                 state_ref, b_ref, *, layer, ts):
    t = pl.program_id(2)
    c64 = CHUNK
    sb = SUBLANES

    @pl.when(t == 0)
    def _():
        state_ref[...] = jnp.zeros_like(state_ref)

    lbl = lbl_ref[...]
    e = jnp.exp(lbl - jnp.max(lbl, axis=0, keepdims=True))
    sm = e / jnp.sum(e, axis=0, keepdims=True)
    lb = jnp.zeros((1, LANES), F32)
    for r in range(1, layer + 1):
        lb = lb + sm[r:r + 1]

    row = lax.broadcasted_iota(jnp.int32, (c64, c64), 0)
    col = lax.broadcasted_iota(jnp.int32, (c64, c64), 1)
    tri = jnp.where(row >= col, 1.0, 0.0).astype(BF16)
    ones = jnp.ones((LANES, LANES), BF16)
    sub_row = lax.broadcasted_iota(jnp.int32, (sb, LANES), 0)
    col8 = lax.broadcasted_iota(jnp.int32, (sb, c64), 1)
    groups = (2 * sb, 4 * sb, 8 * sb)
    level_masks = [(row // grp == col // grp) & (row % grp >= grp // 2) & (col % grp < grp // 2)
                   for grp in groups]


    def gates_and_cumsum(r0, u):
        fl = fl_ref[0, pl.ds(r0, c64), :]
        f = lb + (1.0 - lb) * jax.nn.sigmoid(fl)
        lf = jnp.log2(f)
        b = jnp.zeros((c64, LANES), F32)
        for piece in _split3(lf):
            b = b + jnp.dot(tri, piece, preferred_element_type=F32)
        b_ref[u] = b
        return dict(q=q_ref[0, pl.ds(r0, c64), :], kk=1.0 - f, b=b,
                    i16=i_ref[0, pl.ds(r0, c64), :].astype(BF16))

    def diagonal_sums(ch, u):
        w_rows = []
        for blk in range(c64 // sb):
            b_blk = ch["b"][blk * sb:(blk + 1) * sb]
            q_blk = ch["q"][blk * sb:(blk + 1) * sb]
            for s_ in range(sb):
                src = blk * sb + s_
                dec = jnp.exp2(b_blk - b_ref[u, src:src + 1, :])
                w_rows.append(jnp.where(sub_row >= s_,
                                        q_blk * dec * ch["kk"][src:src + 1, :], 0.0))
        w_all = jnp.concatenate(w_rows, axis=0).astype(BF16)
        ch["sums"] = jnp.dot(w_all, ones, preferred_element_type=F32)

    def level_scores(ch, u):
        q, kk, b = ch["q"], ch["kk"], ch["b"]
        out = []
        for grp in groups:
            parts = []
            for gidx in range(c64 // grp):
                edge = gidx * grp + grp // 2 - 1
                parts.append(jnp.broadcast_to(b_ref[u, edge:edge + 1, :], (grp, LANES)))
            ref = parts[0] if len(parts) == 1 else jnp.concatenate(parts, axis=0)
            qd = (q * jnp.exp2(jnp.minimum(b - ref, 0.0))).astype(BF16)
            kd = (kk * jnp.exp2(jnp.minimum(ref - b, 0.0))).astype(BF16)
            out.append(lax.dot_general(qd, kd, NT_DIMS, preferred_element_type=F32))
        ch["levels"] = out
        b_last = b_ref[u, c64 - 1:c64, :]
        kd_end = (kk * jnp.exp2(b_last - b)).astype(BF16)
        ch["upd"] = lax.dot_general(ch["i16"], kd_end, TN_DIMS,
                                    preferred_element_type=F32)
        ch["decay"] = jnp.exp2(b_last)
        ch["q_in"] = (q * jnp.exp2(b)).astype(BF16)

    def intra_chunk(ch):
        score_rows = []
        for blk in range(c64 // sb):
            acc = jnp.zeros((sb, c64), F32)
            for s_ in range(sb):
                idx = blk * sb + s_
                acc = jnp.where(col8 == idx, ch["sums"][idx * sb:(idx + 1) * sb, :c64], acc)
            score_rows.append(acc)
        scores = jnp.concatenate(score_rows, axis=0)
        for lvl in range(len(groups)):
            scores = scores + jnp.where(level_masks[lvl], ch["levels"][lvl], 0.0)
        ch["intra"] = jnp.dot(scores.astype(BF16), ch["i16"], preferred_element_type=F32)

    per_step = b_ref.shape[0]

    def step(c, carry):
        base = c * (per_step * c64)
        starts = [pl.multiple_of(base + u * c64, c64) for u in range(per_step)]
        chunks = [gates_and_cumsum(r0, u) for u, r0 in enumerate(starts)]
        for u, ch in enumerate(chunks):
            diagonal_sums(ch, u)
        for u, ch in enumerate(chunks):
            level_scores(ch, u)
        for ch in chunks:
            intra_chunk(ch)
        st = state_ref[...]
        for r0, ch in zip(starts, chunks):
            inter = lax.dot_general(ch["q_in"], st.astype(BF16), NT_DIMS,
                                    preferred_element_type=F32)
            st = st * ch["decay"] + ch["upd"]
            o = inter + ch["intra"]
            ms = jnp.mean(o * o, axis=-1, keepdims=True)
            on = (o * lax.rsqrt(ms + RMS_EPS)) * ong_ref[...]
            gv = g_ref[0, pl.ds(r0, c64), :]
            o_ref[0, pl.ds(r0, c64), :] = (on * (gv * jax.nn.sigmoid(gv))).astype(o_ref.dtype)
        state_ref[...] = st
        return carry

    lax.fori_loop(0, ts // (per_step * c64), step, 0)


def _hgrn_core(proj, lb_logits, o_norm_g, layer, n_heads):
    bsz, s, d4 = proj.shape
    d = d4 // 4
    assert d // n_heads == LANES
    ts = _tile(s, HGRN_TIME_TILE)
    per_step = math.gcd(HGRN_CHUNKS_PER_STEP, ts // CHUNK)
    depth = lb_logits.shape[0]
    return pl.pallas_call(
        functools.partial(_hgrn_kernel, layer=layer, ts=ts),
        grid=(bsz, n_heads, s // ts),
        in_specs=[pl.BlockSpec((1, ts, LANES), lambda b, h, t: (b, t, h)),
                  pl.BlockSpec((1, ts, LANES), lambda b, h, t: (b, t, n_heads + h)),
                  pl.BlockSpec((1, ts, LANES), lambda b, h, t: (b, t, 2 * n_heads + h)),
                  pl.BlockSpec((1, ts, LANES), lambda b, h, t: (b, t, 3 * n_heads + h)),
                  pl.BlockSpec((depth, LANES), lambda b, h, t: (0, h)),
                  pl.BlockSpec((1, LANES), lambda b, h, t: (0, 0))],
        out_specs=pl.BlockSpec((1, ts, LANES), lambda b, h, t: (b, t, h)),
        out_shape=jax.ShapeDtypeStruct((bsz, s, d), BF16),
        scratch_shapes=[pltpu.VMEM((LANES, LANES), F32),
                        pltpu.VMEM((per_step, CHUNK, LANES), F32)],
        compiler_params=_params("parallel", "parallel", "arbitrary"),
        name="hgrn2",
    )(proj, proj, proj, proj, lb_logits, o_norm_g.reshape(1, LANES))


def kernel(x, c, ada_w, ada_b, norm_g, ffn_w_gate, ffn_w_up, ffn_w_down,
           attn_w_in, attn_w_out, attn_q_gain, attn_k_gain, attn_lambda, attn_subln_g,
           conv_w_in, conv_w, conv_w_out,
           hgrn_w_in, hgrn_w_out, hgrn_o_norm_g, hgrn_lb_logits):
    depth = ada_w.shape[0]
    d = x.shape[-1]
    qk_dim = attn_q_gain.shape[-1]
    attn_heads = d // (2 * qk_dim)
    hgrn_heads = d // hgrn_o_norm_g.shape[-1]
    ffn_w_gate, ffn_w_up, ffn_w_down = (w.astype(BF16) for w in (ffn_w_gate, ffn_w_up, ffn_w_down))
    attn_w_in, attn_w_out = attn_w_in.astype(BF16), attn_w_out.astype(BF16)
    conv_w_in, conv_w_out = conv_w_in.astype(BF16), conv_w_out.astype(BF16)
    hgrn_w_in, hgrn_w_out = hgrn_w_in.astype(BF16), hgrn_w_out.astype(BF16)

    mod = _ada_mod(c, ada_w, ada_b)
    slopes2 = jnp.asarray(
        [2.0 ** (-8.0 * (h + 1) / attn_heads) * LOG2E for h in range(attn_heads)], F32)
    q_scale = qk_dim ** -0.5 * LOG2E

    for layer in range(depth):
        x = _ffn(x, mod, norm_g, ffn_w_gate, ffn_w_up, ffn_w_down, layer, 0)
        kind, slot = layer % N_MIXERS, layer // N_MIXERS
        if kind == 0:
            lambda_init = 0.8 - 0.6 * math.exp(-0.3 * layer)
            gains = jnp.stack([attn_q_gain[slot], attn_k_gain[slot]])
            qkv = _proj_in(x, mod, norm_g, attn_w_in, layer, slot, BF16,
                           gains=gains, q_scale=q_scale)
            logit_bound = (1.02 * qk_dim * q_scale * jnp.max(jnp.abs(attn_q_gain[slot]))
                           * jnp.max(jnp.abs(attn_k_gain[slot])))
            y = _attention(qkv, slopes2, logit_bound, attn_lambda[slot], attn_subln_g[slot],
                           lambda_init, attn_heads)
            x = _proj_out(y, attn_w_out, x, mod, layer, slot)
        elif kind == 1:
            y = _conv_in(x, mod, norm_g, conv_w_in, conv_w, layer, slot)
            x = _proj_out(y, conv_w_out, x, mod, layer, slot)
        else:
            proj = _proj_in(x, mod, norm_g, hgrn_w_in, layer, slot, F32)
            y = _hgrn_core(proj, hgrn_lb_logits, hgrn_o_norm_g[slot], layer, hgrn_heads)
            x = _proj_out(y, hgrn_w_out, x, mod, layer, slot)
        x = _ffn(x, mod, norm_g, ffn_w_gate, ffn_w_up, ffn_w_down, layer, 1)
    return x
```
